```python
import jax, jax.numpy as jnp
from jax import lax
import numpy as np

D_MODEL = 1024
BATCH = 4
SEQ = 8192
DEPTH = 1
DEC_BATCH = 128
DEC_SEQ = 8
PAST_LEN = 8192
PAGE_SIZE = 128

HEAD_DIM = 64
D_MIX = D_MODEL
D_NSA = D_MIX // 2
D_SB = D_MIX - D_NSA
H_NSA = D_NSA // HEAD_DIM
H_SB = D_SB // HEAD_DIM
KV_NSA = 2
GQA = H_NSA // KV_NSA
CMP_BLK = 32
SEL_BLK = 64
TOP_N = 16
WINDOW = 512
Q_BLK = 128
D_FF = ((8 * D_MODEL // 3 + 127) // 128) * 128
N_NSA_KV = 4
IN_DIM = D_NSA + N_NSA_KV * KV_NSA * HEAD_DIM + 2 * KV_NSA * HEAD_DIM + 3 * H_NSA + 3 * D_SB
EPS = 1e-6
NEG = -1e30
FORCED = 1e6
SCALE = HEAD_DIM ** -0.5

kernel_name = 'hymba_nsa_stickbreak_macaron_step'


def rms_norm(x, w):
    xf = x.astype(jnp.float32)
    y = xf * lax.rsqrt(jnp.mean(xf * xf, axis=-1, keepdims=True) + EPS)
    return (y * w.astype(jnp.float32)).astype(x.dtype)


def swiglu(h, w_gu, w_down):
    g, u = jnp.split(h @ w_gu, 2, axis=-1)
    return (jax.nn.silu(g) * u) @ w_down


def masked_softmax(s, valid):
    s = jnp.where(valid, s, NEG)
    m = jnp.max(s, axis=-1, keepdims=True)
    e = jnp.where(valid, jnp.exp(s - m), 0.0)
    return e / jnp.maximum(jnp.sum(e, axis=-1, keepdims=True), 1e-30)


def alibi_slopes():
    h = jnp.arange(1, H_NSA + 1, dtype=jnp.float32)
    return jnp.exp2(-8.0 * h / H_NSA).reshape(KV_NSA, GQA)


def project(h, w_in, q_norm, k_norm):
    B, T, _ = h.shape
    p = h @ w_in
    o1 = D_NSA
    o2 = o1 + N_NSA_KV * KV_NSA * HEAD_DIM
    o3 = o2 + 2 * KV_NSA * HEAD_DIM
    o4 = o3 + 3 * H_NSA
    q_a = rms_norm(p[..., :o1].reshape(B, T, KV_NSA, GQA, HEAD_DIM), q_norm)
    nsa = p[..., o1:o2].reshape(B, T, N_NSA_KV, KV_NSA, HEAD_DIM)
    nsa = nsa.at[:, :, 2].set(rms_norm(nsa[:, :, 2], k_norm[1]))
    win = p[..., o2:o3].reshape(B, T, 2, KV_NSA, HEAD_DIM)
    win = win.at[:, :, 0].set(rms_norm(win[:, :, 0], k_norm[2]))
    gates = jax.nn.sigmoid(p[..., o3:o4].astype(jnp.float32)).reshape(B, T, 3, KV_NSA, GQA)
    q_b = p[..., o4:o4 + D_SB].reshape(B, T, H_SB, HEAD_DIM)
    sb = p[..., o4 + D_SB:].reshape(B, T, 2, H_SB, HEAD_DIM)
    return q_a, nsa, win, gates, q_b, sb


def compress(cmp_k, cmp_v, cmp_pos_w, k_norm_cmp):
    B, Lp = cmp_k.shape[:2]
    shape = (B, Lp // CMP_BLK, CMP_BLK, KV_NSA, HEAD_DIM)
    w = jax.nn.softmax(cmp_pos_w.astype(jnp.float32), axis=0).astype(cmp_k.dtype)
    kc = jnp.einsum('bnjkd,jk->bnkd', cmp_k.reshape(shape), w)
    vc = jnp.einsum('bnjkd,jk->bnkd', cmp_v.reshape(shape), w)
    return rms_norm(kc, k_norm_cmp), vc


def cmp_branch(q, q_pos, kc, vc, slopes):
    n_cmp = kc.shape[1]
    s = jnp.einsum('btkgd,bnkd->bkgtn', q, kc).astype(jnp.float32) * SCALE
    end = (jnp.arange(n_cmp) + 1) * CMP_BLK - 1
    dist = q_pos[:, None] - end[None, :]
    s = s - slopes[:, :, None, None] * dist.astype(jnp.float32)
    p = masked_softmax(s, dist >= 0)
    o = jnp.einsum('bkgtn,bnkd->btkgd', p.astype(vc.dtype), vc)
    imp = p.sum(axis=2)
    B, K, T, _ = imp.shape
    r = SEL_BLK // CMP_BLK
    imp = imp.reshape(B, K, T, n_cmp // r, r).sum(-1)
    return o, imp


def sel_branch(q, q_pos, sel_k, sel_v, imp, slopes):
    B, Lp = sel_k.shape[:2]
    n_sel = Lp // SEL_BLK
    k_top = min(TOP_N, n_sel)
    blk = jnp.arange(n_sel)[None, :]
    cur = (q_pos // SEL_BLK)[:, None]
    forced = (blk == 0) | (blk == cur) | (blk == cur - 1)
    avail = blk * SEL_BLK <= q_pos[:, None]
    score = jnp.where(avail, jnp.where(forced, FORCED, imp), NEG)
    _, idx = lax.top_k(score, k_top)
    kb = sel_k.reshape(B, n_sel, SEL_BLK, KV_NSA, HEAD_DIM).transpose(0, 3, 1, 2, 4)
    vb = sel_v.reshape(B, n_sel, SEL_BLK, KV_NSA, HEAD_DIM).transpose(0, 3, 1, 2, 4)
    take = jax.vmap(jax.vmap(lambda a, i: a[i]))
    ks = take(kb, idx)
    vs = take(vb, idx)
    s = jnp.einsum('btkgd,bktnjd->bkgtnj', q, ks).astype(jnp.float32) * SCALE
    kpos = idx[..., None] * SEL_BLK + jnp.arange(SEL_BLK)
    dist = (q_pos[:, None, None] - kpos)[:, :, None]
    s = s - slopes[:, :, None, None, None] * dist.astype(jnp.float32)
    flat = s.shape[:4] + (k_top * SEL_BLK,)
    valid = jnp.broadcast_to(dist >= 0, s.shape).reshape(flat)
    p = masked_softmax(s.reshape(flat), valid).reshape(s.shape)
    return jnp.einsum('bkgtnj,bktnjd->btkgd', p.astype(vs.dtype), vs)


def win_branch(q, q_pos, wk, wv, k_pos, slopes):
    s = jnp.einsum('btkgd,bnkd->bkgtn', q, wk).astype(jnp.float32) * SCALE
    dist = q_pos[:, None] - k_pos[None, :]
    valid = (dist >= 0) & (dist <= WINDOW) & (k_pos[None, :] >= 0)
    s = s - slopes[:, :, None, None] * dist.astype(jnp.float32)
    p = masked_softmax(s, valid)
    return jnp.einsum('bkgtn,bnkd->btkgd', p.astype(wv.dtype), wv)


def sb_branch(q, k, v, q_pos, k_pos):
    z = jnp.einsum('bthd,bnhd->bhtn', q, k).astype(jnp.float32) * SCALE
    before = k_pos[None, :] < q_pos[:, None]
    log_1mb = jnp.where(before, jax.nn.log_sigmoid(-z), 0.0)
    c = jnp.cumsum(log_1mb, axis=-1)
    log_a = jax.nn.log_sigmoid(z) + c[..., -1:] - c
    a = jnp.where(before, jnp.exp(log_a), 0.0)
    return jnp.einsum('bhtn,bnhd->bthd', a.astype(v.dtype), v)


def mix_core(q_a, gates, q_b, q_pos, kc, vc, sel_k, sel_v, wk, wv, w_pos, sb_k, sb_v, sb_pos, slopes):
    o_cmp, imp = cmp_branch(q_a, q_pos, kc, vc, slopes)
    o_sel = sel_branch(q_a, q_pos, sel_k, sel_v, imp, slopes)
    o_win = win_branch(q_a, q_pos, wk, wv, w_pos, slopes)
    g = gates.astype(o_cmp.dtype)[..., None]
    o_nsa = g[:, :, 0] * o_cmp + g[:, :, 1] * o_sel + g[:, :, 2] * o_win
    o_sb = sb_branch(q_b, sb_k, sb_v, q_pos, sb_pos)
    B, T = q_a.shape[:2]
    return o_nsa.reshape(B, T, D_NSA), o_sb.reshape(B, T, D_SB)


def prompt_mixer(h, w_in, q_norm, k_norm, cmp_pos_w, slopes):
    B, S, _ = h.shape
    q_a, nsa, win, gates, q_b, sb = project(h, w_in, q_norm, k_norm)
    Lp = -(-S // SEL_BLK) * SEL_BLK
    nsa_p = jnp.pad(nsa, ((0, 0), (0, Lp - S), (0, 0), (0, 0), (0, 0)))
    kc, vc = compress(nsa_p[:, :, 0], nsa_p[:, :, 1], cmp_pos_w, k_norm[0])
    win_p = jnp.pad(win, ((0, 0), (WINDOW, 0), (0, 0), (0, 0), (0, 0)))
    sb_pos = jnp.arange(S)

    def block(i):
        t0 = i * Q_BLK
        q_pos = t0 + jnp.arange(Q_BLK)
        sl = lambda a: lax.dynamic_slice_in_dim(a, t0, Q_BLK, axis=1)
        wkv = lax.dynamic_slice_in_dim(win_p, t0, Q_BLK + WINDOW, axis=1)
        w_pos = t0 - WINDOW + jnp.arange(Q_BLK + WINDOW)
        return mix_core(sl(q_a), sl(gates), sl(q_b), q_pos, kc, vc, nsa_p[:, :, 2], nsa_p[:, :, 3],
                        wkv[:, :, 0], wkv[:, :, 1], w_pos, sb[:, :, 0], sb[:, :, 1], sb_pos, slopes)

    o_nsa, o_sb = lax.map(block, jnp.arange(S // Q_BLK))
    o_nsa = o_nsa.transpose(1, 0, 2, 3).reshape(B, S, D_NSA)
    o_sb = o_sb.transpose(1, 0, 2, 3).reshape(B, S, D_SB)
    return o_nsa, o_sb, nsa, win[:, S - min(WINDOW, S):], sb


def sample_mixer(h, cache_nsa, state_win, cache_sb, page_table, w_in, q_norm, k_norm, cmp_pos_w, slopes):
    DB, T, _ = h.shape
    past = page_table.shape[1] * cache_nsa.shape[1]
    L = past + T
    Lp = -(-L // SEL_BLK) * SEL_BLK
    wb = state_win.shape[1]
    q_a, nsa, win, gates, q_b, sb = project(h, w_in, q_norm, k_norm)
    win_all = jnp.concatenate([state_win.astype(win.dtype), win], axis=1)
    w_pos = past - wb + jnp.arange(wb + T)
    q_pos = past + jnp.arange(T)
    sb_pos = jnp.arange(L)

    def one(args):
        q_a_i, gates_i, q_b_i, nsa_i, win_i, sb_i, pages = args
        nsa_past = cache_nsa[pages].reshape(past, N_NSA_KV, KV_NSA, HEAD_DIM).astype(nsa_i.dtype)
        nsa_full = jnp.concatenate([nsa_past, nsa_i], axis=0)
        nsa_full = jnp.pad(nsa_full, ((0, Lp - L), (0, 0), (0, 0), (0, 0)))[None]
        kc, vc = compress(nsa_full[:, :, 0], nsa_full[:, :, 1], cmp_pos_w, k_norm[0])
        sb_past = cache_sb[pages].reshape(past, 2, H_SB, HEAD_DIM).astype(sb_i.dtype)
        sb_full = jnp.concatenate([sb_past, sb_i], axis=0)[None]
        o_nsa, o_sb = mix_core(q_a_i[None], gates_i[None], q_b_i[None], q_pos, kc, vc,
                               nsa_full[:, :, 2], nsa_full[:, :, 3], win_i[None, :, 0], win_i[None, :, 1],
                               w_pos, sb_full[:, :, 0], sb_full[:, :, 1], sb_pos, slopes)
        return o_nsa[0], o_sb[0]

    o_nsa, o_sb = lax.map(one, (q_a, gates, q_b, nsa, win_all, sb, page_table))
    return o_nsa, o_sb, nsa, win_all[:, -wb:], sb


def merge_out(o_nsa, o_sb, out_norm_a, out_norm_b, w_o):
    return jnp.concatenate([rms_norm(o_nsa, out_norm_a), rms_norm(o_sb, out_norm_b)], axis=-1) @ w_o


def setup_inputs(seed: int = 0) -> dict:
    key = jax.random.key(seed)
    ks = jax.random.split(key, 24)
    n_pages = PAST_LEN // PAGE_SIZE
    n_pool = (DEC_BATCH * n_pages * 5) // 4
    wb = min(WINDOW, PAST_LEN)
    nrm = lambda k, shape, s=1.0: jax.random.normal(k, shape, jnp.float32) * s
    gain = lambda k, shape: 1.0 + 0.02 * jax.random.normal(k, shape, jnp.float32)
    page_table = jax.random.permutation(ks[5], n_pool)[:DEC_BATCH * n_pages].reshape(DEC_BATCH, n_pages).astype(jnp.int32)
    return {
        'x_prompt': nrm(ks[0], (BATCH, SEQ, D_MODEL)),
        'x_sample': nrm(ks[1], (DEC_BATCH, DEC_SEQ, D_MODEL)),
        'cache_nsa': nrm(ks[2], (DEPTH, n_pool, PAGE_SIZE, N_NSA_KV, KV_NSA, HEAD_DIM)),
        'state_win': nrm(ks[3], (DEPTH, DEC_BATCH, wb, 2, KV_NSA, HEAD_DIM)),
        'cache_sb': nrm(ks[4], (DEPTH, n_pool, PAGE_SIZE, 2, H_SB, HEAD_DIM)),
        'page_table': page_table,
        'norm_ffn1': gain(ks[6], (DEPTH, D_MODEL)),
        'w_ffn1_gu': nrm(ks[7], (DEPTH, D_MODEL, 2 * D_FF), D_MODEL ** -0.5),
        'w_ffn1_down': nrm(ks[8], (DEPTH, D_FF, D_MODEL), D_FF ** -0.5),
        'norm_mix': gain(ks[9], (DEPTH, D_MODEL)),
        'w_in': nrm(ks[10], (DEPTH, D_MODEL, IN_DIM), D_MODEL ** -0.5),
        'q_norm': gain(ks[11], (DEPTH, HEAD_DIM)),
        'k_norm': gain(ks[12], (DEPTH, 3, HEAD_DIM)),
        'cmp_pos_w': nrm(ks[13], (DEPTH, CMP_BLK, KV_NSA), 0.1),
        'out_norm_a': gain(ks[14], (DEPTH, D_NSA)),
        'out_norm_b': gain(ks[15], (DEPTH, D_SB)),
        'w_o': nrm(ks[16], (DEPTH, D_MIX, D_MODEL), D_MIX ** -0.5),
        'norm_ffn2': gain(ks[17], (DEPTH, D_MODEL)),
        'w_ffn2_gu': nrm(ks[18], (DEPTH, D_MODEL, 2 * D_FF), D_MODEL ** -0.5),
        'w_ffn2_down': nrm(ks[19], (DEPTH, D_FF, D_MODEL), D_FF ** -0.5),
    }


def reference(x_prompt, x_sample, cache_nsa, state_win, cache_sb, page_table, norm_ffn1, w_ffn1_gu, w_ffn1_down,
              norm_mix, w_in, q_norm, k_norm, cmp_pos_w, out_norm_a, out_norm_b, w_o, norm_ffn2, w_ffn2_gu, w_ffn2_down):
    slopes = alibi_slopes()
    xp, xs = x_prompt, x_sample
    nsa_p, nsa_s, win_p, win_s, sb_p, sb_s = [], [], [], [], [], []
    for l in range(DEPTH):
        xp = xp + 0.5 * swiglu(rms_norm(xp, norm_ffn1[l]), w_ffn1_gu[l], w_ffn1_down[l])
        xs = xs + 0.5 * swiglu(rms_norm(xs, norm_ffn1[l]), w_ffn1_gu[l], w_ffn1_down[l])
        a_p, b_p, n1, n2, n3 = prompt_mixer(rms_norm(xp, norm_mix[l]), w_in[l], q_norm[l], k_norm[l], cmp_pos_w[l], slopes)
        a_s, b_s, m1, m2, m3 = sample_mixer(rms_norm(xs, norm_mix[l]), cache_nsa[l], state_win[l], cache_sb[l], page_table,
                                            w_in[l], q_norm[l], k_norm[l], cmp_pos_w[l], slopes)
        xp = xp + merge_out(a_p, b_p, out_norm_a[l], out_norm_b[l], w_o[l])
        xs = xs + merge_out(a_s, b_s, out_norm_a[l], out_norm_b[l], w_o[l])
        xp = xp + 0.5 * swiglu(rms_norm(xp, norm_ffn2[l]), w_ffn2_gu[l], w_ffn2_down[l])
        xs = xs + 0.5 * swiglu(rms_norm(xs, norm_ffn2[l]), w_ffn2_gu[l], w_ffn2_down[l])
        nsa_p.append(n1); win_p.append(n2); sb_p.append(n3)
        nsa_s.append(m1); win_s.append(m2); sb_s.append(m3)
    return (xp, xs, jnp.stack(nsa_p), jnp.stack(nsa_s), jnp.stack(win_p), jnp.stack(win_s), jnp.stack(sb_p), jnp.stack(sb_s))
```

```python
import functools

import numpy as np
import jax
import jax.numpy as jnp
from jax import lax
from jax.experimental import pallas as pl
from jax.experimental.pallas import tpu as pltpu

F32 = jnp.float32
BF16 = jnp.bfloat16

HEAD_DIM = 64
KV_NSA = 2
GQA = 4
N_SLAB = KV_NSA * GQA
CMP_BLK = 32
SEL_BLK = 64
TOP_N = 16
WINDOW = 512
EPS = 1e-6
NEG = -1e30
FORCED = 1e6
SCALE = HEAD_DIM ** -0.5
LANES = 128
PAGE = 128

VMEM_LIMIT = 56 * 1024 * 1024

TM_FFN = 512
FF_CHUNK = 256
TQ_CMP = 128
TQ_NSA = 128
TK_NSA = 256
TQ_SB = 256
P_PAGES = 8

NT_DIMS = (((1,), (1,)), ((), ()))
TN_DIMS = (((0,), (0,)), ((), ()))


def _cparams(sem):
    return pltpu.CompilerParams(dimension_semantics=sem, vmem_limit_bytes=VMEM_LIMIT)


def _rms(x, w):
    ms = jnp.mean(x * x, axis=-1, keepdims=True)
    return x * lax.rsqrt(ms + EPS) * w


def _split_dot(x, m):
    hi = x.astype(BF16)
    lo = (x - hi.astype(F32)).astype(BF16)
    return (jnp.dot(hi, m, preferred_element_type=F32) + jnp.dot(lo, m, preferred_element_type=F32))


def _group_rms(x, gsum, w):
    ms = _split_dot(x * x, gsum) * (1.0 / HEAD_DIM)
    return x * lax.rsqrt(ms + EPS) * w


def _softplus(z):
    return jnp.maximum(z, 0.0) + jnp.log(1.0 + jnp.exp(-jnp.abs(z)))


def _ffn_body(x_ref, g_ref, wgu_ref, wd_ref, o_ref, acc_ref, *, d_ff):
    x = x_ref[...]
    h = _rms(x, g_ref[...]).astype(BF16)
    for c in range(d_ff // FF_CHUNK):
        lo = c * FF_CHUNK
        g = jnp.dot(h, wgu_ref[:, lo:lo + FF_CHUNK], preferred_element_type=F32)
        u = jnp.dot(h, wgu_ref[:, d_ff + lo:d_ff + lo + FF_CHUNK], preferred_element_type=F32)
        a = (g / (1.0 + jnp.exp(-g)) * u).astype(BF16)
        d = jnp.dot(a, wd_ref[lo:lo + FF_CHUNK, :], preferred_element_type=F32)
        if c == 0:
            acc_ref[...] = d
        else:
            acc_ref[...] += d
    o_ref[...] = x + 0.5 * acc_ref[...]


def _ffn(x, gain, wgu, wd):
    n, d = x.shape
    d_ff = wd.shape[0]
    tm = min(TM_FFN, n)
    const = lambda i: (0, 0)
    return pl.pallas_call(
        functools.partial(_ffn_body, d_ff=d_ff),
        out_shape=jax.ShapeDtypeStruct((n, d), F32),
        grid=(n // tm,),
        in_specs=[
            pl.BlockSpec((tm, d), lambda i: (i, 0)),
            pl.BlockSpec((1, d), const),
            pl.BlockSpec((d, 2 * d_ff), const, pipeline_mode=pl.Buffered(1)),
            pl.BlockSpec((d_ff, d), const, pipeline_mode=pl.Buffered(1)),
        ],
        out_specs=pl.BlockSpec((tm, d), lambda i: (i, 0)),
        scratch_shapes=[pltpu.VMEM((tm, d), F32)],
        compiler_params=_cparams(("parallel",)),
        name="ffn",
    )(x, gain, wgu, wd)


def _inproj_body(x_ref, g_ref, w_ref, qn_ref, ksel_ref, kwin_ref, gsum_ref,
                 qa_ref, nsa_ref, selkv_ref, win_ref, winb_ref, qb_ref, sb_ref, sbb_ref, gates_ref):
    h = _rms(x_ref[...], g_ref[...]).astype(BF16)
    gsum = gsum_ref[...]
    dot = lambda lo, hi: jnp.dot(h, w_ref[:, lo:hi], preferred_element_type=F32)

    qn = qn_ref[...]
    for j in range(4):
        pq = dot(j * LANES, (j + 1) * LANES)
        qa_ref[:, j * LANES:(j + 1) * LANES] = (_group_rms(pq, gsum, qn) * SCALE).astype(BF16)

    pn = dot(512, 1024)
    selk = _group_rms(pn[:, 256:384], gsum, ksel_ref[...])
    nsa_ref[:, 0:256] = pn[:, 0:256]
    nsa_ref[:, 256:384] = selk
    nsa_ref[:, 384:512] = pn[:, 384:512]
    selkv_ref[:, 0:128] = selk.astype(BF16)
    selkv_ref[:, 128:256] = pn[:, 384:512].astype(BF16)

    pw = dot(1024, 1280)
    wk = _group_rms(pw[:, 0:128], gsum, kwin_ref[...])
    win_ref[:, 0:128] = wk
    win_ref[:, 128:256] = pw[:, 128:256]
    winb_ref[:, 0:128] = wk.astype(BF16)
    winb_ref[:, 128:256] = pw[:, 128:256].astype(BF16)

    qb_ref[...] = (dot(1280, 1792) * SCALE).astype(BF16)
    ps = dot(1792, 2816)
    sb_ref[...] = ps
    sbb_ref[...] = ps.astype(BF16)
    pg = dot(2816, 2944)
    gates_ref[...] = 1.0 / (1.0 + jnp.exp(-pg))


def _inproj(x, gain, w_packed, qn, ksel, kwin, gsum):
    n, d = x.shape
    tm = min(TM_FFN, n)
    const = lambda i: (0, 0)
    row = lambda i: (i, 0)
    widths = [(512, BF16), (512, F32), (256, BF16), (256, F32), (256, BF16), (512, BF16),
              (1024, F32), (1024, BF16), (128, F32)]
    return pl.pallas_call(
        _inproj_body,
        out_shape=[jax.ShapeDtypeStruct((n, w), dt) for w, dt in widths],
        grid=(n // tm,),
        in_specs=[
            pl.BlockSpec((tm, d), row),
            pl.BlockSpec((1, d), const),
            pl.BlockSpec(w_packed.shape, const, pipeline_mode=pl.Buffered(1)),
            pl.BlockSpec((1, LANES), const),
            pl.BlockSpec((1, LANES), const),
            pl.BlockSpec((1, LANES), const),
            pl.BlockSpec((LANES, LANES), const),
        ],
        out_specs=[pl.BlockSpec((tm, w), row) for w, _ in widths],
        compiler_params=_cparams(("parallel",)),
        name="inproj",
    )(x, gain, w_packed, qn, ksel, kwin, gsum)


def _compress_body(pt_ref, *refs):
    pages = refs[:P_PAGES]
    w_ref, kn_ref, gsum_ref, out_ref = refs[P_PAGES:]
    x = jnp.concatenate([r[...] for r in pages], axis=0)
    n = x.shape[0] // CMP_BLK
    c = jnp.sum(x.reshape(n, CMP_BLK, 2 * LANES) * w_ref[...][None], axis=1)
    kc = _group_rms(c[:, 0:LANES], gsum_ref[...], kn_ref[...])
    out_ref[:, 0:LANES] = kc.astype(BF16)
    out_ref[:, LANES:2 * LANES] = c[:, LANES:2 * LANES].astype(BF16)


def _compress(cache, page_table, w_cmp, kn, gsum):
    nseq, npages = page_table.shape
    rows = P_PAGES * PAGE // CMP_BLK

    def page_map(i):
        return lambda s, j, pt: (pt[s * npages + j * P_PAGES + i], 0, 0)

    const = lambda s, j, pt: (0, 0)
    grid_spec = pltpu.PrefetchScalarGridSpec(
        num_scalar_prefetch=1,
        grid=(nseq, npages // P_PAGES),
        in_specs=[pl.BlockSpec((None, PAGE, 2 * LANES), page_map(i)) for i in range(P_PAGES)] + [
            pl.BlockSpec((CMP_BLK, 2 * LANES), const),
            pl.BlockSpec((1, LANES), const),
            pl.BlockSpec((LANES, LANES), const),
        ],
        out_specs=pl.BlockSpec((None, rows, 2 * LANES), lambda s, j, pt: (s, j, 0)),
    )
    return pl.pallas_call(
        _compress_body,
        out_shape=jax.ShapeDtypeStruct((nseq, npages * PAGE // CMP_BLK, 2 * LANES), BF16),
        grid_spec=grid_spec,
        compiler_params=_cparams(("parallel", "arbitrary")),
        name="compress",
    )(page_table.reshape(-1), *([cache] * P_PAGES), w_cmp, kn, gsum)


def _cmp_topk_body(q_ref, kcvc_ref, pair_ref, ocmp_ref, selb_ref, *, tq, pos_base, k_lanes):
    n_cmp = kcvc_ref.shape[0]
    q = q_ref[...]
    kc = kcvc_ref[:, 0:LANES]
    vc = kcvc_ref[:, LANES:2 * LANES]
    qpos = pos_base + pl.program_id(1) * tq + lax.broadcasted_iota(jnp.int32, (tq, 1), 0)
    end = (lax.broadcasted_iota(jnp.int32, (1, n_cmp), 1) + 1) * CMP_BLK - 1
    dist = qpos - end
    valid = dist >= 0
    distf = dist.astype(F32)
    lane = lax.broadcasted_iota(jnp.int32, (1, LANES), 1)
    lo_half = lane < HEAD_DIM
    lanef = lane.astype(F32)
    n_blk = n_cmp * CMP_BLK // SEL_BLK
    cur = jnp.right_shift(qpos, 6)
    forced = (lane == 0) | (lane == cur) | (lane == cur - 1)
    avail = (lane * SEL_BLK <= qpos) & (lane < n_blk)

    outs = [[None] * KV_NSA for _ in range(GQA)]
    for kv in range(KV_NSA):
        half = lo_half if kv == 0 else jnp.logical_not(lo_half)
        imp = jnp.zeros((tq, n_cmp), F32)
        for g in range(GQA):
            qm = jnp.where(half, q[:, g * LANES:(g + 1) * LANES], jnp.zeros((), BF16))
            s = lax.dot_general(qm, kc, NT_DIMS, preferred_element_type=F32)
            s = s - (2.0 ** -(kv * GQA + g + 1)) * distf
            s = jnp.where(valid, s, NEG)
            m = jnp.max(s, axis=-1, keepdims=True)
            e = jnp.where(valid, jnp.exp(s - m), 0.0)
            p = e / jnp.maximum(jnp.sum(e, axis=-1, keepdims=True), 1e-30)
            outs[g][kv] = jnp.dot(p.astype(BF16), vc, preferred_element_type=F32)
            imp = imp + p
        score = jnp.where(avail, jnp.where(forced, FORCED, _split_dot(imp, pair_ref[...])), NEG)
        sel = jnp.zeros((tq, LANES), F32)
        for _ in range(k_lanes):
            mx = jnp.max(score, axis=-1, keepdims=True)
            idx = jnp.min(jnp.where(score == mx, lanef, 1e9), axis=-1, keepdims=True)
            hit = lanef == idx
            sel = jnp.where(hit, 1.0, sel)
            score = jnp.where(hit, -3e38, score)
        selb_ref[kv] = jnp.where(sel > 0.5, 0.0, NEG).astype(BF16)
    for g in range(GQA):
        ocmp_ref[:, g * LANES:(g + 1) * LANES] = jnp.where(lo_half, outs[g][0], outs[g][1])


def _cmp_topk(q_a, kcvc, pair, *, tq, pos_base, k_lanes):
    nseq, t, _ = q_a.shape
    n_cmp = kcvc.shape[1]
    return pl.pallas_call(
        functools.partial(_cmp_topk_body, tq=tq, pos_base=pos_base, k_lanes=k_lanes),
        out_shape=[jax.ShapeDtypeStruct((nseq, t, 4 * LANES), F32),
                   jax.ShapeDtypeStruct((nseq, KV_NSA, t, LANES), BF16)],
        grid=(nseq, t // tq),
        in_specs=[
            pl.BlockSpec((None, tq, 4 * LANES), lambda b, i: (b, i, 0)),
            pl.BlockSpec((None, n_cmp, 2 * LANES), lambda b, i: (b, 0, 0)),
            pl.BlockSpec(pair.shape, lambda b, i: (0, 0)),
        ],
        out_specs=[pl.BlockSpec((None, tq, 4 * LANES), lambda b, i: (b, i, 0)),
                   pl.BlockSpec((None, KV_NSA, tq, LANES), lambda b, i: (b, 0, i, 0))],
        compiler_params=_cparams(("parallel", "parallel")),
        name="cmp_topk",
    )(q_a, kcvc, pair)


def _build_qaug(q, selb_ref, qaug_ref, tq):
    lane = lax.broadcasted_iota(jnp.int32, (1, LANES), 1)
    lo_half = lane < HEAD_DIM
    for kv in range(KV_NSA):
        half = lo_half if kv == 0 else jnp.logical_not(lo_half)
        for g in range(GQA):
            r0 = (kv * GQA + g) * tq
            qaug_ref[r0:r0 + tq, 0:LANES] = jnp.where(half, q[:, g * LANES:(g + 1) * LANES],
                                                      jnp.zeros((), BF16))
            qaug_ref[r0:r0 + tq, LANES:2 * LANES] = selb_ref[kv]


def _flash_update(s, vt, kposrel, qrel, m_ref, l_ref, acc_ref, p_ref, tq):
    kf = kposrel.astype(F32)
    for i in range(N_SLAB):
        rows = slice(i * tq, (i + 1) * tq)
        si = s[rows] + (2.0 ** -(i + 1)) * kf
        if qrel is not None:
            si = jnp.where(kposrel <= qrel, si, NEG)
        m_prev = m_ref[rows]
        m_new = jnp.maximum(m_prev, jnp.max(si, axis=-1, keepdims=True))
        alpha = jnp.exp(m_prev - m_new)
        p = jnp.exp(si - m_new)
        l_ref[rows] = alpha * l_ref[rows] + jnp.sum(p, axis=-1, keepdims=True)
        acc_ref[rows] = alpha * acc_ref[rows]
        m_ref[rows] = m_new
        p_ref[rows] = p.astype(BF16)
    acc_ref[...] += jnp.dot(p_ref[...], vt, preferred_element_type=F32)


def _window_attn(qw, kw, vw, valid_fn, tq):
    s = lax.dot_general(qw, kw, NT_DIMS, preferred_element_type=F32)
    nk = kw.shape[0]
    j = lax.broadcasted_iota(jnp.int32, (1, nk), 1)
    r = lax.broadcasted_iota(jnp.int32, (tq, 1), 0)
    valid = valid_fn(j, r)
    jf = j.astype(F32)
    ps, ls = [], []
    for i in range(N_SLAB):
        si = s[i * tq:(i + 1) * tq] + (2.0 ** -(i + 1)) * jf
        si = jnp.where(valid, si, NEG)
        m = jnp.max(si, axis=-1, keepdims=True)
        e = jnp.where(valid, jnp.exp(si - m), 0.0)
        ls.append(jnp.maximum(jnp.sum(e, axis=-1, keepdims=True), 1e-30))
        ps.append(e.astype(BF16))
    o = jnp.dot(jnp.concatenate(ps, axis=0), vw, preferred_element_type=F32)
    return o / jnp.concatenate(ls, axis=0)


def _combine(o_sel, o_win, ocmp, gates, out_ref, tq):
    lane = lax.broadcasted_iota(jnp.int32, (1, LANES), 1)
    lo_half = lane < HEAD_DIM
    for g in range(GQA):
        def pick(a):
            return jnp.where(lo_half, a[g * tq:(g + 1) * tq], a[(GQA + g) * tq:(GQA + g + 1) * tq])

        def gate(br):
            c0 = br * N_SLAB + g
            c1 = br * N_SLAB + GQA + g
            return jnp.where(lo_half, gates[:, c0:c0 + 1], gates[:, c1:c1 + 1])

        out_ref[:, g * LANES:(g + 1) * LANES] = (gate(0) * ocmp[:, g * LANES:(g + 1) * LANES]
                                                 + gate(1) * pick(o_sel) + gate(2) * pick(o_win))


def _nsa_prompt_body(flags_ref, q_ref, gates_ref, ocmp_ref, selb_ref, selkv_ref, winkv_ref, oh_ref,
                     out_ref, qaug_ref, m_ref, l_ref, acc_ref, p_ref, *, tq, tk, nqb, nkt):
    b = pl.program_id(0)
    qb = pl.program_id(1)
    t0 = pl.multiple_of(qb * tq, tq)
    _build_qaug(q_ref[...], selb_ref, qaug_ref, tq)
    m_ref[...] = jnp.full(m_ref.shape, NEG, F32)
    l_ref[...] = jnp.zeros(l_ref.shape, F32)
    acc_ref[...] = jnp.zeros(acc_ref.shape, F32)
    qrel = lax.broadcasted_iota(jnp.int32, (tq, 1), 0)
    n_t = (t0 + tq + tk - 1) // tk

    def tile(i, carry):
        c = n_t - 1 - i

        @pl.when(flags_ref[(b * nqb + qb) * nkt + c] > 0)
        def _():
            k0 = pl.multiple_of(c * tk, tk)
            kaug = jnp.concatenate([selkv_ref[pl.ds(k0, tk), 0:LANES], oh_ref[pl.ds(k0, tk), :]], axis=1)
            s = lax.dot_general(qaug_ref[...], kaug, NT_DIMS, preferred_element_type=F32)
            kposrel = (k0 - t0) + lax.broadcasted_iota(jnp.int32, (1, tk), 1)
            _flash_update(s, selkv_ref[pl.ds(k0, tk), LANES:2 * LANES], kposrel, qrel,
                          m_ref, l_ref, acc_ref, p_ref, tq)
        return carry

    lax.fori_loop(0, n_t, tile, 0)
    o_sel = acc_ref[...] / l_ref[...]

    nw = tq + WINDOW
    kw = winkv_ref[pl.ds(t0, nw), 0:LANES]
    vw = winkv_ref[pl.ds(t0, nw), LANES:2 * LANES]
    valid_fn = lambda j, r: (j >= r) & (j <= r + WINDOW) & (j + t0 >= WINDOW)
    o_win = _window_attn(qaug_ref[:, 0:LANES], kw, vw, valid_fn, tq)
    _combine(o_sel, o_win, ocmp_ref[...], gates_ref[...], out_ref, tq)


def _nsa_prompt(flags, q_a, gates, ocmp, selb, selkv, winkv_pad, onehot):
    nb, s, _ = q_a.shape
    tq, tk = TQ_NSA, TK_NSA
    nqb, nkt = s // tq, s // tk
    rows = N_SLAB * tq
    qmap = lambda b, i, f: (b, i, 0)
    grid_spec = pltpu.PrefetchScalarGridSpec(
        num_scalar_prefetch=1,
        grid=(nb, nqb),
        in_specs=[
            pl.BlockSpec((None, tq, 4 * LANES), qmap),
            pl.BlockSpec((None, tq, LANES), qmap),
            pl.BlockSpec((None, tq, 4 * LANES), qmap),
            pl.BlockSpec((None, KV_NSA, tq, LANES), lambda b, i, f: (b, 0, i, 0)),
            pl.BlockSpec((None, s, 2 * LANES), lambda b, i, f: (b, 0, 0)),
            pl.BlockSpec((None, s + WINDOW, 2 * LANES), lambda b, i, f: (b, 0, 0)),
            pl.BlockSpec((s, LANES), lambda b, i, f: (0, 0)),
        ],
        out_specs=pl.BlockSpec((None, tq, 4 * LANES), qmap),
        scratch_shapes=[
            pltpu.VMEM((rows, 2 * LANES), BF16),
            pltpu.VMEM((rows, 1), F32),
            pltpu.VMEM((rows, 1), F32),
            pltpu.VMEM((rows, LANES), F32),
            pltpu.VMEM((rows, tk), BF16),
        ],
    )
    return pl.pallas_call(
        functools.partial(_nsa_prompt_body, tq=tq, tk=tk, nqb=nqb, nkt=nkt),
        out_shape=jax.ShapeDtypeStruct((nb, s, 4 * LANES), F32),
        grid_spec=grid_spec,
        compiler_params=_cparams(("parallel", "arbitrary")),
        name="nsa_prompt",
    )(flags, q_a, gates, ocmp, selb, selkv, winkv_pad, onehot)


def _sb_prompt_body(q_ref, k_ref, v_ref, u_ref, out_ref, qs_ref, r_ref, acc_ref, *, tq):
    qb = pl.program_id(2)
    lane = lax.broadcasted_iota(jnp.int32, (1, LANES), 1)
    lo_half = lane < HEAD_DIM
    q = q_ref[...]
    zero = jnp.zeros((), BF16)
    qs_ref[0:tq] = jnp.where(lo_half, q, zero)
    qs_ref[tq:2 * tq] = jnp.where(lo_half, zero, q)
    r_ref[...] = jnp.zeros(r_ref.shape, F32)
    acc_ref[...] = jnp.zeros(acc_ref.shape, F32)

    def tile(c, diag):
        k0 = pl.multiple_of(c * tq, tq)
        z = lax.dot_general(qs_ref[...], k_ref[pl.ds(k0, tq), :], NT_DIMS, preferred_element_type=F32)
        sp = _softplus(z)
        if diag:
            row = lax.broadcasted_iota(jnp.int32, (tq, tq), 0)
            col = lax.broadcasted_iota(jnp.int32, (tq, tq), 1)
            before = jnp.concatenate([col < row, col < row], axis=0)
            nl = jnp.where(before, sp, 0.0)
        else:
            nl = sp
        a = jnp.exp((z - sp) - (r_ref[...] + _split_dot(nl, u_ref[...])))
        if diag:
            a = jnp.where(before, a, 0.0)
        acc_ref[...] += jnp.dot(a.astype(BF16), v_ref[pl.ds(k0, tq), :], preferred_element_type=F32)
        r_ref[...] += jnp.sum(nl, axis=-1, keepdims=True)

    tile(qb, True)

    def body(i, carry):
        tile(qb - 1 - i, False)
        return carry

    lax.fori_loop(0, qb, body, 0)
    out_ref[...] = jnp.where(lo_half, acc_ref[0:tq], acc_ref[tq:2 * tq])


def _sb_prompt(q_b, sbb, umat):
    nb, s, _ = q_b.shape
    tq = TQ_SB
    npair = q_b.shape[2] // LANES
    return pl.pallas_call(
        functools.partial(_sb_prompt_body, tq=tq),
        out_shape=jax.ShapeDtypeStruct((nb, s, npair * LANES), F32),
        grid=(nb, npair, s // tq),
        in_specs=[
            pl.BlockSpec((None, tq, LANES), lambda b, j, i: (b, i, j)),
            pl.BlockSpec((None, s, LANES), lambda b, j, i: (b, 0, j)),
            pl.BlockSpec((None, s, LANES), lambda b, j, i: (b, 0, npair + j)),
            pl.BlockSpec((tq, tq), lambda b, j, i: (0, 0)),
        ],
        out_specs=pl.BlockSpec((None, tq, LANES), lambda b, j, i: (b, i, j)),
        scratch_shapes=[
            pltpu.VMEM((2 * tq, LANES), BF16),
            pltpu.VMEM((2 * tq, 1), F32),
            pltpu.VMEM((2 * tq, LANES), F32),
        ],
        compiler_params=_cparams(("parallel", "parallel", "arbitrary")),
        name="sb_prompt",
    )(q_b, sbb, sbb, umat)


def _nsa_sample_body(pt_ref, *refs, t, npages, past):
    pages = refs[:P_PAGES]
    (qaug_ref, new_ref, winkv_ref, gates_ref, ocmp_ref, out_ref, m_ref, l_ref, acc_ref, p_ref) = refs[P_PAGES:]
    step = pl.program_id(1)
    nsteps = npages // P_PAGES
    tk = P_PAGES * PAGE

    @pl.when(step == 0)
    def _():
        m_ref[...] = jnp.full(m_ref.shape, NEG, F32)
        l_ref[...] = jnp.zeros(l_ref.shape, F32)
        acc_ref[...] = jnp.zeros(acc_ref.shape, F32)
        s = lax.dot_general(qaug_ref[:, 0:LANES], new_ref[:, 0:LANES], NT_DIMS, preferred_element_type=F32)
        j = lax.broadcasted_iota(jnp.int32, (1, LANES), 1)
        qrel = lax.broadcasted_iota(jnp.int32, (t, 1), 0)
        s = jnp.where(j < t, s, NEG)
        _flash_update(s, new_ref[:, LANES:2 * LANES], j, qrel, m_ref, l_ref, acc_ref,
                      p_ref.at[:, 0:LANES], t)

    pg0 = npages - (step + 1) * P_PAGES
    x = jnp.concatenate([r[...] for r in pages], axis=0)
    row = lax.broadcasted_iota(jnp.int32, (tk, LANES), 0)
    lane = lax.broadcasted_iota(jnp.int32, (tk, LANES), 1)
    onehot = jnp.where(lane == 2 * pg0 + jnp.right_shift(row, 6), 1.0, 0.0).astype(BF16)
    kaug = jnp.concatenate([x[:, 0:LANES].astype(BF16), onehot], axis=1)
    s = lax.dot_general(qaug_ref[...], kaug, NT_DIMS, preferred_element_type=F32)
    kposrel = (pg0 * PAGE - past) + lax.broadcasted_iota(jnp.int32, (1, tk), 1)
    _flash_update(s, x[:, LANES:2 * LANES].astype(BF16), kposrel, None, m_ref, l_ref, acc_ref, p_ref, t)

    @pl.when(step == nsteps - 1)
    def _():
        o_sel = acc_ref[...] / l_ref[...]
        wb = WINDOW
        valid_fn = lambda j, r: (j >= r) & (j <= r + WINDOW) & (j < wb + t)
        o_win = _window_attn(qaug_ref[:, 0:LANES], winkv_ref[:, 0:LANES], winkv_ref[:, LANES:2 * LANES],
                             valid_fn, t)
        _combine(o_sel, o_win, ocmp_ref[...], gates_ref[...], out_ref, t)


def _nsa_sample(cache, page_table, qaug, selnew, winkv, gates, ocmp, *, past):
    nseq, npages = page_table.shape
    t = gates.shape[1]
    rows = N_SLAB * t
    tk = P_PAGES * PAGE

    def page_map(i):
        return lambda s, j, pt: (pt[s * npages + npages - (j + 1) * P_PAGES + i], 0, 1)

    seq = lambda s, j, pt: (s, 0, 0)
    grid_spec = pltpu.PrefetchScalarGridSpec(
        num_scalar_prefetch=1,
        grid=(nseq, npages // P_PAGES),
        in_specs=[pl.BlockSpec((None, PAGE, 2 * LANES), page_map(i)) for i in range(P_PAGES)] + [
            pl.BlockSpec((None, rows, 2 * LANES), seq),
            pl.BlockSpec((None, LANES, 2 * LANES), seq),
            pl.BlockSpec((None, winkv.shape[1], 2 * LANES), seq),
            pl.BlockSpec((None, t, LANES), seq),
            pl.BlockSpec((None, t, 4 * LANES), seq),
        ],
        out_specs=pl.BlockSpec((None, t, 4 * LANES), seq),
        scratch_shapes=[
            pltpu.VMEM((rows, 1), F32),
            pltpu.VMEM((rows, 1), F32),
            pltpu.VMEM((rows, LANES), F32),
            pltpu.VMEM((rows, tk), BF16),
        ],
    )
    return pl.pallas_call(
        functools.partial(_nsa_sample_body, t=t, npages=npages, past=past),
        out_shape=jax.ShapeDtypeStruct((nseq, t, 4 * LANES), F32),
        grid_spec=grid_spec,
        compiler_params=_cparams(("parallel", "arbitrary")),
        name="nsa_sample",
    )(page_table.reshape(-1), *([cache] * P_PAGES), qaug, selnew, winkv, gates, ocmp)


def _sb_sample_body(pt_ref, *refs, t, nh):
    pages = refs[:P_PAGES]
    qbd_ref, new_ref, u_ref, out_ref, r_ref, acc_ref = refs[P_PAGES:]
    step = pl.program_id(1)
    nsteps = pl.num_programs(1)
    d = nh * HEAD_DIM
    qbd = qbd_ref[...]
    umat = u_ref[...]

    def page(kp, vp, before):
        z = jnp.dot(kp, qbd, preferred_element_type=F32)
        sp = _softplus(z)
        nl = sp if before is None else jnp.where(before, sp, 0.0)
        hi = nl.astype(BF16)
        lo = (nl - hi.astype(F32)).astype(BF16)
        wn = jnp.dot(umat, hi, preferred_element_type=F32) + jnp.dot(umat, lo, preferred_element_type=F32)
        a = jnp.exp((z - sp) - (r_ref[...] + wn))
        if before is not None:
            a = jnp.where(before, a, 0.0)
        acc_ref[...] += lax.dot_general(a.astype(BF16), vp, TN_DIMS, preferred_element_type=F32)
        r_ref[...] += jnp.sum(nl, axis=0, keepdims=True)

    @pl.when(step == 0)
    def _():
        r_ref[...] = jnp.zeros(r_ref.shape, F32)
        acc_ref[...] = jnp.zeros(acc_ref.shape, F32)
        j = lax.broadcasted_iota(jnp.int32, (PAGE, LANES), 0)
        c = lax.broadcasted_iota(jnp.int32, (PAGE, LANES), 1)
        before = (j < (c & (t - 1))) & (j < t)
        page(new_ref[:, 0:d], new_ref[:, d:2 * d], before)

    for i in reversed(range(P_PAGES)):
        xp = pages[i]
        page(xp[:, 0:d].astype(BF16), xp[:, d:2 * d].astype(BF16), None)

    @pl.when(step == nsteps - 1)
    def _():
        col = jnp.right_shift(lax.broadcasted_iota(jnp.int32, (t, d), 1), 6)
        o = jnp.zeros((t, d), F32)
        for h in range(nh):
            o = o + jnp.where(col == h, acc_ref[h * t:(h + 1) * t, :], 0.0)
        out_ref[...] = o


def _sb_sample(cache, page_table, qbd, sbnew, umat, *, t):
    nseq, npages = page_table.shape
    d2 = cache.shape[2]
    d = d2 // 2
    nh = d // HEAD_DIM

    def page_map(i):
        return lambda s, j, pt: (pt[s * npages + npages - (j + 1) * P_PAGES + i], 0, 0)

    seq = lambda s, j, pt: (s, 0, 0)
    grid_spec = pltpu.PrefetchScalarGridSpec(
        num_scalar_prefetch=1,
        grid=(nseq, npages // P_PAGES),
        in_specs=[pl.BlockSpec((None, PAGE, d2), page_map(i)) for i in range(P_PAGES)] + [
            pl.BlockSpec((None, d, LANES), seq),
            pl.BlockSpec((None, PAGE, d2), seq),
            pl.BlockSpec((PAGE, PAGE), lambda s, j, pt: (0, 0)),
        ],
        out_specs=pl.BlockSpec((None, t, d), seq),
        scratch_shapes=[pltpu.VMEM((1, LANES), F32), pltpu.VMEM((LANES, d), F32)],
    )
    return pl.pallas_call(
        functools.partial(_sb_sample_body, t=t, nh=nh),
        out_shape=jax.ShapeDtypeStruct((nseq, t, d), F32),
        grid_spec=grid_spec,
        compiler_params=_cparams(("parallel", "arbitrary")),
        name="sb_sample",
    )(page_table.reshape(-1), *([cache] * P_PAGES), qbd, sbnew, umat)


def _outproj_body(x_ref, oa_ref, ob_ref, ga_ref, gb_ref, wa_ref, wb_ref, o_ref):
    a = _rms(oa_ref[...], ga_ref[...]).astype(BF16)
    b = _rms(ob_ref[...], gb_ref[...]).astype(BF16)
    o_ref[...] = (x_ref[...] + jnp.dot(a, wa_ref[...], preferred_element_type=F32)
                  + jnp.dot(b, wb_ref[...], preferred_element_type=F32))


def _outproj(x, oa, ob, ga, gb, wa, wb):
    n, d = x.shape
    da, db = oa.shape[1], ob.shape[1]
    tm = min(TM_FFN, n)
    const = lambda i: (0, 0)
    row = lambda i: (i, 0)
    return pl.pallas_call(
        _outproj_body,
        out_shape=jax.ShapeDtypeStruct((n, d), F32),
        grid=(n // tm,),
        in_specs=[
            pl.BlockSpec((tm, d), row), pl.BlockSpec((tm, da), row), pl.BlockSpec((tm, db), row),
            pl.BlockSpec((1, da), const), pl.BlockSpec((1, db), const),
            pl.BlockSpec((da, d), const), pl.BlockSpec((db, d), const),
        ],
        out_specs=pl.BlockSpec((tm, d), row),
        compiler_params=_cparams(("parallel",)),
        name="outproj",
    )(x, oa, ob, ga, gb, wa, wb)


def _qa_perm():
    idx = np.zeros((KV_NSA * GQA * HEAD_DIM,), np.int32)
    for kv in range(KV_NSA):
        for g in range(GQA):
            for d in range(HEAD_DIM):
                idx[LANES * g + HEAD_DIM * kv + d] = (kv * GQA + g) * HEAD_DIM + d
    return idx


def _layer(xp, xs, cache_nsa, state_win, cache_sb, page_table, p):
    nb, s, d = xp.shape
    db, t, _ = xs.shape
    npages = page_table.shape[1]
    past = npages * PAGE
    assert s % TK_NSA == 0 and s % TQ_SB == 0 and s // SEL_BLK <= LANES and s >= WINDOW
    assert past // SEL_BLK <= LANES and t & (t - 1) == 0 and npages % P_PAGES == 0 and state_win.shape[1] == WINDOW
    assert t * (p['out_norm_b'].shape[0] // HEAD_DIM) <= LANES and t <= SEL_BLK
    d_nsa = KV_NSA * GQA * HEAD_DIM
    d_sb = p['out_norm_b'].shape[0]

    perm = _qa_perm()
    o1, o2, o3, o4 = 512, 1024, 1280, 1304
    cols = np.concatenate([perm, np.arange(o1, o3), np.arange(o4, o4 + 3 * d_sb), np.arange(o3, o4)])
    w_packed = jnp.pad(p['w_in'][:, cols], ((0, 0), (0, LANES - (o4 - o3)))).astype(BF16)
    tile2 = lambda v: jnp.tile(v, 2).reshape(1, LANES)
    qn, kn_cmp, kn_sel, kn_win = tile2(p['q_norm']), tile2(p['k_norm'][0]), tile2(p['k_norm'][1]), tile2(p['k_norm'][2])
    gidx = np.arange(LANES) // HEAD_DIM
    gsum = jnp.asarray(gidx[:, None] == gidx[None, :], BF16)
    w_sm = jax.nn.softmax(p['cmp_pos_w'].astype(F32), axis=0)
    w_cmp = jnp.tile(jnp.repeat(w_sm, HEAD_DIM, axis=1), (1, 2))
    ga = p['out_norm_a'][perm].reshape(1, d_nsa)
    gb = p['out_norm_b'].reshape(1, d_sb)
    wo_a = p['w_o'][:d_nsa][perm].astype(BF16)
    wo_b = p['w_o'][d_nsa:].astype(BF16)
    wgu1, wd1 = p['w_ffn1_gu'].astype(BF16), p['w_ffn1_down'].astype(BF16)
    wgu2, wd2 = p['w_ffn2_gu'].astype(BF16), p['w_ffn2_down'].astype(BF16)
    g1, gm, g2 = (p[k].reshape(1, d) for k in ('norm_ffn1', 'norm_mix', 'norm_ffn2'))

    def pair_mat(n_cmp):
        return jnp.asarray(np.arange(n_cmp)[:, None] // 2 == np.arange(LANES)[None, :], BF16)

    xp2 = xp.reshape(nb * s, d)
    xs2 = xs.reshape(db * t, d)

    xp2 = _ffn(xp2, g1, wgu1, wd1)
    xs2 = _ffn(xs2, g1, wgu1, wd1)
    proj_p = _inproj(xp2, gm, w_packed, qn, kn_sel, kn_win, gsum)
    proj_s = _inproj(xs2, gm, w_packed, qn, kn_sel, kn_win, gsum)
    qa_p, nsa_p, selkv_p, win_p, winb_p, qb_p, sb_p, sbb_p, gates_p = (a.reshape(nb, s, -1) for a in proj_p)
    qa_s, nsa_s, selkv_s, win_s, winb_s, qb_s, sb_s, sbb_s, gates_s = (a.reshape(db, t, -1) for a in proj_s)

    ident_pt = jnp.arange(nb * s // PAGE, dtype=jnp.int32).reshape(nb, s // PAGE)
    kcvc_p = _compress(nsa_p.reshape(nb * s // PAGE, PAGE, 4 * LANES), ident_pt, w_cmp, kn_cmp, gsum)
    ocmp_p, selb_p = _cmp_topk(qa_p, kcvc_p, pair_mat(s // CMP_BLK), tq=TQ_CMP, pos_base=0, k_lanes=TOP_N)
    nqb, nkt = s // TQ_NSA, s // TK_NSA
    picked = (selb_p == 0).reshape(nb, KV_NSA, nqb, TQ_NSA, LANES)
    picked = jnp.any(picked, axis=(1, 3))[..., :s // SEL_BLK]
    flags = jnp.any(picked.reshape(nb, nqb, nkt, TK_NSA // SEL_BLK), axis=-1).astype(jnp.int32).reshape(-1)
    onehot = jnp.asarray(np.arange(s)[:, None] // SEL_BLK == np.arange(LANES)[None, :], BF16)
    winkv_pad = jnp.pad(winb_p, ((0, 0), (WINDOW, 0), (0, 0)))
    onsa_p = _nsa_prompt(flags, qa_p, gates_p, ocmp_p, selb_p, selkv_p, winkv_pad, onehot)

    umat_p = jnp.asarray(np.arange(TQ_SB)[:, None] > np.arange(TQ_SB)[None, :], BF16)
    osb_p = _sb_prompt(qb_p, sbb_p, umat_p)

    cache_nsa2 = cache_nsa.reshape(cache_nsa.shape[0], PAGE, 4 * LANES)
    kcvc_s = _compress(cache_nsa2, page_table, w_cmp, kn_cmp, gsum)
    k_lanes = TOP_N - 1
    ocmp_s, selb_s = _cmp_topk(qa_s, kcvc_s, pair_mat(past // CMP_BLK), tq=t, pos_base=past, k_lanes=k_lanes)
    lane_lo = (np.arange(LANES) < HEAD_DIM)
    halfmask = jnp.asarray(np.stack([lane_lo, ~lane_lo]), BF16)
    q4 = qa_s.reshape(db, t, GQA, LANES).transpose(0, 2, 1, 3)
    qrows = q4[:, None] * halfmask[None, :, None, None, :]
    brows = jnp.broadcast_to(selb_s[:, :, None], (db, KV_NSA, GQA, t, LANES))
    qaug_s = jnp.concatenate([qrows, brows], axis=-1).reshape(db, N_SLAB * t, 2 * LANES)
    selnew = jnp.pad(selkv_s, ((0, 0), (0, LANES - t), (0, 0)))
    win_all = jnp.concatenate([state_win.reshape(db, WINDOW, 2 * LANES), win_s], axis=1)
    winkv_s = jnp.pad(win_all.astype(BF16), ((0, 0), (0, LANES - t), (0, 0)))
    onsa_s = _nsa_sample(cache_nsa2, page_table, qaug_s, selnew, winkv_s, gates_s, ocmp_s, past=past)

    nh = d_sb // HEAD_DIM
    cache_sb2 = cache_sb.reshape(cache_sb.shape[0], PAGE, 2 * d_sb)
    eye = jnp.eye(nh, dtype=BF16)
    qh = qb_s.reshape(db, t, nh, HEAD_DIM)
    qbd = jnp.einsum('bthd,hg->bhdgt', qh, eye).reshape(db, d_sb, nh * t)
    qbd = jnp.pad(qbd, ((0, 0), (0, 0), (0, LANES - nh * t)))
    sbnew = jnp.pad(sbb_s, ((0, 0), (0, PAGE - t), (0, 0)))
    umat_s = jnp.asarray(np.arange(PAGE)[None, :] > np.arange(PAGE)[:, None], BF16)
    osb_s = _sb_sample(cache_sb2, page_table, qbd, sbnew, umat_s, t=t)

    xp2 = _outproj(xp2, onsa_p.reshape(nb * s, d_nsa), osb_p.reshape(nb * s, d_sb), ga, gb, wo_a, wo_b)
    xs2 = _outproj(xs2, onsa_s.reshape(db * t, d_nsa), osb_s.reshape(db * t, d_sb), ga, gb, wo_a, wo_b)
    xp2 = _ffn(xp2, g2, wgu2, wd2)
    xs2 = _ffn(xs2, g2, wgu2, wd2)

    wb = min(WINDOW, s)
    outs = (nsa_p.reshape(nb, s, 4, KV_NSA, HEAD_DIM),
            nsa_s.reshape(db, t, 4, KV_NSA, HEAD_DIM),
            win_p[:, s - wb:].reshape(nb, wb, 2, KV_NSA, HEAD_DIM),
            win_all[:, t:].reshape(db, WINDOW, 2, KV_NSA, HEAD_DIM),
            sb_p.reshape(nb, s, 2, nh, HEAD_DIM),
            sb_s.reshape(db, t, 2, nh, HEAD_DIM))
    return xp2.reshape(nb, s, d), xs2.reshape(db, t, d), outs


def kernel(x_prompt, x_sample, cache_nsa, state_win, cache_sb, page_table, norm_ffn1, w_ffn1_gu, w_ffn1_down,
           norm_mix, w_in, q_norm, k_norm, cmp_pos_w, out_norm_a, out_norm_b, w_o, norm_ffn2, w_ffn2_gu,
           w_ffn2_down):
    names = ('norm_ffn1', 'w_ffn1_gu', 'w_ffn1_down', 'norm_mix', 'w_in', 'q_norm', 'k_norm', 'cmp_pos_w',
             'out_norm_a', 'out_norm_b', 'w_o', 'norm_ffn2', 'w_ffn2_gu', 'w_ffn2_down')
    params = (norm_ffn1, w_ffn1_gu, w_ffn1_down, norm_mix, w_in, q_norm, k_norm, cmp_pos_w,
              out_norm_a, out_norm_b, w_o, norm_ffn2, w_ffn2_gu, w_ffn2_down)
    depth = w_in.shape[0]
    xp, xs = x_prompt, x_sample
    per_layer = []
    for l in range(depth):
        p = {k: v[l] for k, v in zip(names, params)}
        xp, xs, outs = _layer(xp, xs, cache_nsa[l], state_win[l], cache_sb[l], page_table, p)
        per_layer.append(outs)
    stacked = tuple(jnp.stack([o[i] for o in per_layer]) for i in range(6))
    return (xp, xs) + stacked
```

```python
import functools

import numpy as np
import jax
import jax.numpy as jnp
from jax import lax
from jax.experimental import pallas as pl
from jax.experimental.pallas import tpu as pltpu

F32 = jnp.float32
BF16 = jnp.bfloat16

HEAD_DIM = 64
KV_NSA = 2
GQA = 4
N_SLAB = KV_NSA * GQA
CMP_BLK = 32
SEL_BLK = 64
TOP_N = 16
WINDOW = 512
EPS = 1e-6
NEG = -1e30
FORCED = 1e6
SCALE = HEAD_DIM ** -0.5
LOG2E = 1.4426950408889634
LANES = 128
PAGE = 128

VMEM_LIMIT = 56 * 1024 * 1024

TM_FFN = 512
FF_CHUNK = 256
ROWS_CMP = 1024
TQ_CMP = 128
SEQ_CMP = 8
TQ_NSA = 128
TK_NSA = 256
TQ_SB = 256
RS_SB = 128
UNROLL_NSA = 2
UNROLL_SB = 4
P_PAGES = 16
P_CMP = 16

NT_DIMS = (((1,), (1,)), ((), ()))


def _cparams(sem):
    return pltpu.CompilerParams(dimension_semantics=sem, vmem_limit_bytes=VMEM_LIMIT)


def _rms(x, w):
    ms = jnp.mean(x * x, axis=-1, keepdims=True)
    return x * lax.rsqrt(ms + EPS) * w


def _split(x):
    hi = x.astype(BF16)
    return hi, (x - hi.astype(F32)).astype(BF16)


def _split_dot(x, m):
    hi, lo = _split(x)
    return jnp.dot(hi, m, preferred_element_type=F32) + jnp.dot(lo, m, preferred_element_type=F32)


def _group_rms(x, gsum, w):
    ms = _split_dot(x * x, gsum) * (1.0 / HEAD_DIM)
    return x * lax.rsqrt(ms + EPS) * w


def _softplus2(z):
    neg_abs = lax.bitcast_convert_type(lax.bitcast_convert_type(z, jnp.uint32) | jnp.uint32(0x80000000), F32)
    return jnp.maximum(z, 0.0) + jnp.log2(1.0 + jnp.exp2(neg_abs))


def _ffn_body(x_ref, g_ref, wgu_ref, wd_ref, o_ref, acc_ref, *, d_ff):
    x = x_ref[...]
    h = _rms(x, g_ref[...]).astype(BF16)
    for c in range(d_ff // FF_CHUNK):
        lo = c * FF_CHUNK
        g = jnp.dot(h, wgu_ref[:, lo:lo + FF_CHUNK], preferred_element_type=F32)
        u = jnp.dot(h, wgu_ref[:, d_ff + lo:d_ff + lo + FF_CHUNK], preferred_element_type=F32)
        a = (g / (1.0 + jnp.exp(-g)) * u).astype(BF16)
        d = jnp.dot(a, wd_ref[lo:lo + FF_CHUNK, :], preferred_element_type=F32)
        if c == 0:
            acc_ref[...] = d
        else:
            acc_ref[...] += d
    o_ref[...] = x + 0.5 * acc_ref[...]


def _ffn(x, gain, wgu, wd):
    n, d = x.shape
    d_ff = wd.shape[0]
    tm = min(TM_FFN, n)
    const = lambda i: (0, 0)
    return pl.pallas_call(
        functools.partial(_ffn_body, d_ff=d_ff),
        out_shape=jax.ShapeDtypeStruct((n, d), F32),
        grid=(n // tm,),
        in_specs=[
            pl.BlockSpec((tm, d), lambda i: (i, 0)),
            pl.BlockSpec((1, d), const),
            pl.BlockSpec((d, 2 * d_ff), const, pipeline_mode=pl.Buffered(1)),
            pl.BlockSpec((d_ff, d), const, pipeline_mode=pl.Buffered(1)),
        ],
        out_specs=pl.BlockSpec((tm, d), lambda i: (i, 0)),
        scratch_shapes=[pltpu.VMEM((tm, d), F32)],
        compiler_params=_cparams(("parallel",)),
        name="ffn",
    )(x, gain, wgu, wd)


def _inproj_body(x_ref, g_ref, w_ref, qn_ref, ksel_ref, kwin_ref, gsum_ref,
                 qa_ref, nsa_ref, selkv_ref, win_ref, winb_ref, qb_ref, sb_ref, sbb_ref, gates_ref):
    h = _rms(x_ref[...], g_ref[...]).astype(BF16)
    gsum = gsum_ref[...]
    dot = lambda lo, hi: jnp.dot(h, w_ref[:, lo:hi], preferred_element_type=F32)

    qn = qn_ref[...]
    for j in range(4):
        pq = dot(j * LANES, (j + 1) * LANES)
        qa_ref[:, j * LANES:(j + 1) * LANES] = (_group_rms(pq, gsum, qn) * (SCALE * LOG2E)).astype(BF16)

    pn = dot(512, 1024)
    selk = _group_rms(pn[:, 256:384], gsum, ksel_ref[...])
    nsa_ref[:, 0:256] = pn[:, 0:256]
    nsa_ref[:, 256:384] = selk
    nsa_ref[:, 384:512] = pn[:, 384:512]
    selkv_ref[:, 0:128] = selk.astype(BF16)
    selkv_ref[:, 128:256] = pn[:, 384:512].astype(BF16)

    pw = dot(1024, 1280)
    wk = _group_rms(pw[:, 0:128], gsum, kwin_ref[...])
    win_ref[:, 0:128] = wk
    win_ref[:, 128:256] = pw[:, 128:256]
    winb_ref[:, 0:128] = wk.astype(BF16)
    winb_ref[:, 128:256] = pw[:, 128:256].astype(BF16)

    qb_ref[...] = (dot(1280, 1792) * (SCALE * LOG2E)).astype(BF16)
    ps = dot(1792, 2816)
    sb_ref[...] = ps
    sbb_ref[...] = ps.astype(BF16)
    pg = dot(2816, 2944)
    gates_ref[...] = 1.0 / (1.0 + jnp.exp(-pg))


def _inproj(x, gain, w_packed, qn, ksel, kwin, gsum):
    n, d = x.shape
    tm = min(TM_FFN, n)
    const = lambda i: (0, 0)
    row = lambda i: (i, 0)
    widths = [(512, BF16), (512, F32), (256, BF16), (256, F32), (256, BF16), (512, BF16),
              (1024, F32), (1024, BF16), (128, F32)]
    return pl.pallas_call(
        _inproj_body,
        out_shape=[jax.ShapeDtypeStruct((n, w), dt) for w, dt in widths],
        grid=(n // tm,),
        in_specs=[
            pl.BlockSpec((tm, d), row),
            pl.BlockSpec((1, d), const),
            pl.BlockSpec(w_packed.shape, const, pipeline_mode=pl.Buffered(1)),
            pl.BlockSpec((1, LANES), const),
            pl.BlockSpec((1, LANES), const),
            pl.BlockSpec((1, LANES), const),
            pl.BlockSpec((LANES, LANES), const),
        ],
        out_specs=[pl.BlockSpec((tm, w), row) for w, _ in widths],
        compiler_params=_cparams(("parallel",)),
        name="inproj",
    )(x, gain, w_packed, qn, ksel, kwin, gsum)


def _compress_rows_body(x_ref, w_ref, kn_ref, gsum_ref, out_ref):
    x = x_ref[...]
    n = x.shape[0] // CMP_BLK
    c = jnp.sum(x.reshape(n, CMP_BLK, 2 * LANES) * w_ref[...][None], axis=1)
    kc = _group_rms(c[:, 0:LANES], gsum_ref[...], kn_ref[...])
    out_ref[:, 0:LANES] = kc.astype(BF16)
    out_ref[:, LANES:2 * LANES] = c[:, LANES:2 * LANES].astype(BF16)


def _compress_rows(nsa, w_cmp, kn, gsum):
    nb, s, _ = nsa.shape
    rows = min(ROWS_CMP, s)
    const = lambda b, j: (0, 0)
    return pl.pallas_call(
        _compress_rows_body,
        out_shape=jax.ShapeDtypeStruct((nb, s // CMP_BLK, 2 * LANES), BF16),
        grid=(nb, s // rows),
        in_specs=[
            pl.BlockSpec((None, rows, 2 * LANES), lambda b, j: (b, j, 0)),
            pl.BlockSpec((CMP_BLK, 2 * LANES), const),
            pl.BlockSpec((1, LANES), const),
            pl.BlockSpec((LANES, LANES), const),
        ],
        out_specs=pl.BlockSpec((None, rows // CMP_BLK, 2 * LANES), lambda b, j: (b, j, 0)),
        compiler_params=_cparams(("parallel", "parallel")),
        name="compress_rows",
    )(nsa, w_cmp, kn, gsum)


def _compress_pages_body(pt_ref, *refs):
    pages = refs[:P_CMP]
    wt_ref, sel_ref, kn_ref, gsum_ref, out_ref = refs[P_CMP:]
    wt = wt_ref[...]
    pieces = []
    for k in range(P_CMP // 2):
        acc = None
        for h in range(2):
            hi, lo = _split(pages[2 * k + h][...] * wt)
            r = (lax.dot_general(sel_ref[h], hi, NT_DIMS, preferred_element_type=F32)
                 + lax.dot_general(sel_ref[h], lo, NT_DIMS, preferred_element_type=F32))
            acc = r if acc is None else acc + r
        pieces.append(acc[0:8])
    c = jnp.concatenate(pieces, axis=0)
    kc = _group_rms(c[:, 0:LANES], gsum_ref[...], kn_ref[...])
    out_ref[:, 0:LANES] = kc.astype(BF16)
    out_ref[:, LANES:2 * LANES] = c[:, LANES:2 * LANES].astype(BF16)


def _compress_pages(cache_t, page_table, wt, sel, kn, gsum):
    nseq, npages = page_table.shape
    rows = P_CMP * PAGE // CMP_BLK

    def page_map(i):
        return lambda s, j, pt: (pt[s * npages + j * P_CMP + i], 0, 0)

    const2 = lambda s, j, pt: (0, 0)
    grid_spec = pltpu.PrefetchScalarGridSpec(
        num_scalar_prefetch=1,
        grid=(nseq, npages // P_CMP),
        in_specs=[pl.BlockSpec((None, 2 * LANES, PAGE), page_map(i)) for i in range(P_CMP)] + [
            pl.BlockSpec((2 * LANES, PAGE), const2),
            pl.BlockSpec(sel.shape, lambda s, j, pt: (0, 0, 0)),
            pl.BlockSpec((1, LANES), const2),
            pl.BlockSpec((LANES, LANES), const2),
        ],
        out_specs=pl.BlockSpec((None, rows, 2 * LANES), lambda s, j, pt: (s, j, 0)),
    )
    return pl.pallas_call(
        _compress_pages_body,
        out_shape=jax.ShapeDtypeStruct((nseq, npages * PAGE // CMP_BLK, 2 * LANES), BF16),
        grid_spec=grid_spec,
        compiler_params=_cparams(("parallel", "arbitrary")),
        name="compress_pages",
    )(page_table.reshape(-1), *([cache_t] * P_CMP), wt, sel, kn, gsum)


def _cmp_topk_body(q_ref, kcvc_ref, pair_ref, ocmp_ref, selb_ref, *, nsb, tq, pos_base, k_lanes):
    n_cmp = kcvc_ref.shape[1]
    nr = nsb * tq
    qpos1 = pos_base + pl.program_id(1) * tq + lax.broadcasted_iota(jnp.int32, (tq, 1), 0)
    qpos = jnp.concatenate([qpos1] * nsb, axis=0)
    end = (lax.broadcasted_iota(jnp.int32, (1, n_cmp), 1) + 1) * CMP_BLK - 1
    dist = qpos - end
    valid = dist >= 0
    distf = dist.astype(F32)
    lane = lax.broadcasted_iota(jnp.int32, (1, LANES), 1)
    lo_half = lane < HEAD_DIM
    lanef = lane.astype(F32)
    n_blk = n_cmp * CMP_BLK // SEL_BLK
    cur = jnp.right_shift(qpos, 6)
    forced = (lane == 0) | (lane == cur) | (lane == cur - 1)
    avail = (lane * SEL_BLK <= qpos) & (lane < n_blk)
    seq_rows = [slice(sq * tq, (sq + 1) * tq) for sq in range(nsb)]

    outs = [[None] * KV_NSA for _ in range(GQA)]
    imps = []
    for kv in range(KV_NSA):
        half = lo_half if kv == 0 else jnp.logical_not(lo_half)
        imp = jnp.zeros((nr, n_cmp), F32)
        for g in range(GQA):
            s = jnp.concatenate([
                lax.dot_general(jnp.where(half, q_ref[sq, :, g * LANES:(g + 1) * LANES], jnp.zeros((), BF16)),
                                kcvc_ref[sq, :, 0:LANES], NT_DIMS, preferred_element_type=F32)
                for sq in range(nsb)], axis=0)
            s = s - (LOG2E * 2.0 ** -(kv * GQA + g + 1)) * distf
            s = jnp.where(valid, s, NEG)
            m = jnp.max(s, axis=-1, keepdims=True)
            e = jnp.where(valid, jnp.exp2(s - m), 0.0)
            p = e / jnp.maximum(jnp.sum(e, axis=-1, keepdims=True), 1e-30)
            outs[g][kv] = [jnp.dot(p[seq_rows[sq]].astype(BF16), kcvc_ref[sq, :, LANES:2 * LANES],
                                   preferred_element_type=F32) for sq in range(nsb)]
            imp = imp + p
        imps.append(imp)
    imp = jnp.concatenate(imps, axis=0)
    avail2 = jnp.concatenate([avail] * KV_NSA, axis=0)
    forced2 = jnp.concatenate([forced] * KV_NSA, axis=0)
    score = jnp.where(avail2, jnp.where(forced2, FORCED, _split_dot(imp, pair_ref[...])), NEG)
    sel = jnp.zeros((KV_NSA * nr, LANES), F32)
    for _ in range(k_lanes):
        mx = jnp.max(score, axis=-1, keepdims=True)
        idx = jnp.min(jnp.where(score == mx, lanef, 1e9), axis=-1, keepdims=True)
        hit = lanef == idx
        sel = jnp.where(hit, 1.0, sel)
        score = jnp.where(hit, -3e38, score)
    selb = jnp.where(sel > 0.5, 0.0, NEG)
    for sq in range(nsb):
        for kv in range(KV_NSA):
            selb_ref[sq, kv] = selb[kv * nr + sq * tq:kv * nr + (sq + 1) * tq].astype(BF16)
        for g in range(GQA):
            ocmp_ref[sq, :, g * LANES:(g + 1) * LANES] = jnp.where(lo_half, outs[g][0][sq], outs[g][1][sq])


def _cmp_topk(q_a, kcvc, pair, *, nsb, tq, pos_base, k_lanes):
    nseq, t, _ = q_a.shape
    n_cmp = kcvc.shape[1]
    return pl.pallas_call(
        functools.partial(_cmp_topk_body, nsb=nsb, tq=tq, pos_base=pos_base, k_lanes=k_lanes),
        out_shape=[jax.ShapeDtypeStruct((nseq, t, 4 * LANES), F32),
                   jax.ShapeDtypeStruct((nseq, KV_NSA, t, LANES), BF16)],
        grid=(nseq // nsb, t // tq),
        in_specs=[
            pl.BlockSpec((nsb, tq, 4 * LANES), lambda b, i: (b, i, 0)),
            pl.BlockSpec((nsb, n_cmp, 2 * LANES), lambda b, i: (b, 0, 0)),
            pl.BlockSpec(pair.shape, lambda b, i: (0, 0)),
        ],
        out_specs=[pl.BlockSpec((nsb, tq, 4 * LANES), lambda b, i: (b, i, 0)),
                   pl.BlockSpec((nsb, KV_NSA, tq, LANES), lambda b, i: (b, 0, i, 0))],
        compiler_params=_cparams(("parallel", "parallel")),
        name="cmp_topk",
    )(q_a, kcvc, pair)


def _build_qaug(q, selb_ref, qaug_ref, tq):
    lane = lax.broadcasted_iota(jnp.int32, (1, LANES), 1)
    lo_half = lane < HEAD_DIM
    for kv in range(KV_NSA):
        half = lo_half if kv == 0 else jnp.logical_not(lo_half)
        for g in range(GQA):
            r0 = (kv * GQA + g) * tq
            qaug_ref[r0:r0 + tq, 0:LANES] = jnp.where(half, q[:, g * LANES:(g + 1) * LANES],
                                                      jnp.zeros((), BF16))
            qaug_ref[r0:r0 + tq, LANES:2 * LANES] = selb_ref[kv]


def _flash_update(s, vaug, kposrel, qrel, m_ref, acc_ref, tq, kv, v_pos_minor):
    kf = kposrel.astype(F32)
    tk = s.shape[1]
    rows = slice(kv * GQA * tq, (kv + 1) * GQA * tq)
    m_prev = m_ref[rows]
    m_parts, p_parts = [], []
    for j in range(GQA):
        sub = slice(j * tq, (j + 1) * tq)
        si = s[sub] + (LOG2E * 2.0 ** -(kv * GQA + j + 1)) * kf
        if qrel is not None:
            si = jnp.where(kposrel <= qrel, si, NEG)
        m_new = jnp.maximum(m_prev[sub], jnp.max(si, axis=-1, keepdims=True))
        m_parts.append(m_new)
        p_parts.append(jnp.concatenate(
            [jnp.exp2(si[:, c * LANES:(c + 1) * LANES] - m_new) for c in range(tk // LANES)], axis=1))
    m_new = jnp.concatenate(m_parts, axis=0)
    p = jnp.concatenate(p_parts, axis=0).astype(BF16)
    if v_pos_minor:
        pv = lax.dot_general(p, vaug, NT_DIMS, preferred_element_type=F32)
    else:
        pv = jnp.dot(p, vaug, preferred_element_type=F32)
    acc_ref[rows] = jnp.exp2(m_prev - m_new) * acc_ref[rows] + pv
    m_ref[rows] = m_new


def _normalised(acc_ref):
    acc = acc_ref[...]
    return acc / pltpu.roll(acc, HEAD_DIM, 1)


def _window_attn(qw, kw, vw, valid_fn, tq):
    s = lax.dot_general(qw, kw, NT_DIMS, preferred_element_type=F32)
    nk = kw.shape[0]
    j = lax.broadcasted_iota(jnp.int32, (1, nk), 1)
    r = lax.broadcasted_iota(jnp.int32, (tq, 1), 0)
    valid = valid_fn(j, r)
    jf = j.astype(F32)
    ps, ls = [], []
    for i in range(N_SLAB):
        si = s[i * tq:(i + 1) * tq] + (LOG2E * 2.0 ** -(i + 1)) * jf
        si = jnp.where(valid, si, NEG)
        m = jnp.max(si, axis=-1, keepdims=True)
        e = jnp.where(valid, jnp.exp2(si - m), 0.0)
        ls.append(jnp.maximum(jnp.sum(e, axis=-1, keepdims=True), 1e-30))
        ps.append(e.astype(BF16))
    o = jnp.dot(jnp.concatenate(ps, axis=0), vw, preferred_element_type=F32)
    return o / jnp.concatenate(ls, axis=0)


def _combine(o_sel, o_win, ocmp, gates, out_ref, tq):
    lane = lax.broadcasted_iota(jnp.int32, (1, LANES), 1)
    lo_half = lane < HEAD_DIM
    for g in range(GQA):
        def pick(a):
            return jnp.where(lo_half, a[g * tq:(g + 1) * tq], a[(GQA + g) * tq:(GQA + g + 1) * tq])

        def gate(br):
            c0 = br * N_SLAB + g
            c1 = br * N_SLAB + GQA + g
            return jnp.where(lo_half, gates[:, c0:c0 + 1], gates[:, c1:c1 + 1])

        out_ref[:, g * LANES:(g + 1) * LANES] = (gate(0) * ocmp[:, g * LANES:(g + 1) * LANES]
                                                 + gate(1) * pick(o_sel) + gate(2) * pick(o_win))


def _nsa_prompt_body(cnt_ref, lst_ref, q_ref, gates_ref, ocmp_ref, selb_ref, selkv_ref, winkv_ref, oh_ref,
                     out_ref, qaug_ref, m_ref, acc_ref, *, tq, tk, nqb, nkt):
    b = pl.program_id(0)
    qb = pl.program_id(1)
    t0 = pl.multiple_of(qb * tq, tq)
    _build_qaug(q_ref[...], selb_ref, qaug_ref, tq)
    m_ref[...] = jnp.full(m_ref.shape, NEG, F32)
    acc_ref[...] = jnp.zeros(acc_ref.shape, F32)
    qrel = lax.broadcasted_iota(jnp.int32, (tq, 1), 0)
    lo_half = lax.broadcasted_iota(jnp.int32, (1, LANES), 1) < HEAD_DIM
    one = jnp.ones((), BF16)
    cd = t0 // tk

    def tile(c, kv, causal):
        k0 = pl.multiple_of(c * tk, tk)
        kaug = jnp.concatenate([selkv_ref[pl.ds(k0, tk), 0:LANES], oh_ref[pl.ds(k0, tk), :]], axis=1)
        vt = selkv_ref[pl.ds(k0, tk), LANES:2 * LANES]
        vaug = jnp.where(lo_half, vt, one) if kv == 0 else jnp.where(lo_half, one, vt)
        r0 = kv * GQA * tq
        s = lax.dot_general(qaug_ref[r0:r0 + GQA * tq], kaug, NT_DIMS, preferred_element_type=F32)
        kposrel = (k0 - t0) + lax.broadcasted_iota(jnp.int32, (1, tk), 1)
        _flash_update(s, vaug, kposrel, qrel if causal else None, m_ref, acc_ref, tq, kv, False)

    for kv in range(KV_NSA):
        tile(cd, kv, True)

        slot = (b * nqb + qb) * KV_NSA + kv
        n = cnt_ref[slot]

        def body(i, carry, kv=kv, slot=slot):
            for u in range(UNROLL_NSA):
                tile(lst_ref[slot * nkt + UNROLL_NSA * i + u], kv, False)
            return carry

        lax.fori_loop(0, n // UNROLL_NSA, body, 0)

        def tail(i, carry, kv=kv, slot=slot, n=n):
            tile(lst_ref[slot * nkt + (n // UNROLL_NSA) * UNROLL_NSA + i], kv, False)
            return carry

        lax.fori_loop(0, n % UNROLL_NSA, tail, 0)

    o_sel = _normalised(acc_ref)
    nw = tq + WINDOW
    kw = winkv_ref[pl.ds(t0, nw), 0:LANES]
    vw = winkv_ref[pl.ds(t0, nw), LANES:2 * LANES]
    valid_fn = lambda j, r: (j >= r) & (j <= r + WINDOW) & (j + t0 >= WINDOW)
    o_win = _window_attn(qaug_ref[:, 0:LANES], kw, vw, valid_fn, tq)
    _combine(o_sel, o_win, ocmp_ref[...], gates_ref[...], out_ref, tq)


def _nsa_prompt(counts, lists, q_a, gates, ocmp, selb, selkv, winkv_pad, onehot):
    nb, s, _ = q_a.shape
    tq, tk = TQ_NSA, TK_NSA
    nqb, nkt = s // tq, s // tk
    rows = N_SLAB * tq
    qmap = lambda b, i, c, l: (b, i, 0)
    grid_spec = pltpu.PrefetchScalarGridSpec(
        num_scalar_prefetch=2,
        grid=(nb, nqb),
        in_specs=[
            pl.BlockSpec((None, tq, 4 * LANES), qmap),
            pl.BlockSpec((None, tq, LANES), qmap),
            pl.BlockSpec((None, tq, 4 * LANES), qmap),
            pl.BlockSpec((None, KV_NSA, tq, LANES), lambda b, i, c, l: (b, 0, i, 0)),
            pl.BlockSpec((None, s, 2 * LANES), lambda b, i, c, l: (b, 0, 0)),
            pl.BlockSpec((None, s + WINDOW, 2 * LANES), lambda b, i, c, l: (b, 0, 0)),
            pl.BlockSpec((s, LANES), lambda b, i, c, l: (0, 0)),
        ],
        out_specs=pl.BlockSpec((None, tq, 4 * LANES), qmap),
        scratch_shapes=[
            pltpu.VMEM((rows, 2 * LANES), BF16),
            pltpu.VMEM((rows, LANES), F32),
            pltpu.VMEM((rows, LANES), F32),
        ],
    )
    return pl.pallas_call(
        functools.partial(_nsa_prompt_body, tq=tq, tk=tk, nqb=nqb, nkt=nkt),
        out_shape=jax.ShapeDtypeStruct((nb, s, 4 * LANES), F32),
        grid_spec=grid_spec,
        compiler_params=_cparams(("parallel", "arbitrary")),
        name="nsa_prompt",
    )(counts, lists, q_a, gates, ocmp, selb, selkv, winkv_pad, onehot)


def _sb_prompt_body(q_ref, k_ref, v_ref, u_ref, out_ref, qs_ref, r_ref, acc_ref, *, tq):
    qb = pl.program_id(2)
    lane = lax.broadcasted_iota(jnp.int32, (1, LANES), 1)
    lo_half = lane < HEAD_DIM
    q = q_ref[...]
    zero = jnp.zeros((), BF16)
    qs_ref[0:tq] = jnp.where(lo_half, q, zero)
    qs_ref[tq:2 * tq] = jnp.where(lo_half, zero, q)
    r_ref[...] = jnp.zeros(r_ref.shape, F32)
    acc_ref[...] = jnp.zeros(acc_ref.shape, F32)
    nh = tq // LANES

    def tile(c, diag):
        k0 = pl.multiple_of(c * tq, tq)
        vt = v_ref[pl.ds(k0, tq), :]
        umat = u_ref[...]
        z_all = lax.dot_general(qs_ref[...], k_ref[pl.ds(k0, tq), :], NT_DIMS, preferred_element_type=F32)
        r_all = r_ref[...]
        a_parts, t_parts = [], []
        for sl in range(2 * tq // RS_SB):
            rows = slice(sl * RS_SB, (sl + 1) * RS_SB)
            z = z_all[rows]
            sp = _softplus2(z)
            if diag:
                qi = (sl * RS_SB) % tq + lax.broadcasted_iota(jnp.int32, (RS_SB, 1), 0)
                before = lax.broadcasted_iota(jnp.int32, (1, tq), 1) < qi
                nl = jnp.where(before, sp, 0.0)
            else:
                nl = sp
            wn = jnp.dot(nl.astype(BF16), umat, preferred_element_type=F32)
            r = r_all[rows]
            parts = []
            for h in range(nh):
                cs = slice(h * LANES, (h + 1) * LANES)
                parts.append(jnp.exp2((z[:, cs] - sp[:, cs]) - (r + wn[:, cs])))
            a = jnp.concatenate(parts, axis=1)
            if diag:
                a = jnp.where(before, a, 0.0)
            a_parts.append(a.astype(BF16))
            t_parts.append(jnp.broadcast_to(wn[:, 0:1] + nl[:, 0:1], (RS_SB, LANES)))
        acc_ref[...] += jnp.dot(jnp.concatenate(a_parts, axis=0), vt, preferred_element_type=F32)
        r_ref[...] = r_all + jnp.concatenate(t_parts, axis=0)

    tile(qb, True)

    def body(i, carry):
        for u in range(UNROLL_SB):
            tile(qb - 1 - u - UNROLL_SB * i, False)
        return carry

    lax.fori_loop(0, qb // UNROLL_SB, body, 0)
    rem = qb % UNROLL_SB

    def tail(i, carry):
        tile(rem - 1 - i, False)
        return carry

    lax.fori_loop(0, rem, tail, 0)
    out_ref[...] = jnp.where(lo_half, acc_ref[0:tq], acc_ref[tq:2 * tq])


def _sb_prompt(q_b, sbb, umat):
    nb, s, _ = q_b.shape
    tq = TQ_SB
    npair = q_b.shape[2] // LANES
    return pl.pallas_call(
        functools.partial(_sb_prompt_body, tq=tq),
        out_shape=jax.ShapeDtypeStruct((nb, s, npair * LANES), F32),
        grid=(nb, npair, s // tq),
        in_specs=[
            pl.BlockSpec((None, tq, LANES), lambda b, j, i: (b, i, j)),
            pl.BlockSpec((None, s, LANES), lambda b, j, i: (b, 0, j)),
            pl.BlockSpec((None, s, LANES), lambda b, j, i: (b, 0, npair + j)),
            pl.BlockSpec((tq, tq), lambda b, j, i: (0, 0)),
        ],
        out_specs=pl.BlockSpec((None, tq, LANES), lambda b, j, i: (b, i, j)),
        scratch_shapes=[
            pltpu.VMEM((2 * tq, LANES), BF16),
            pltpu.VMEM((2 * tq, LANES), F32),
            pltpu.VMEM((2 * tq, LANES), F32),
        ],
        compiler_params=_cparams(("parallel", "parallel", "arbitrary")),
        name="sb_prompt",
    )(q_b, sbb, sbb, umat)


def _nsa_sample_body(pt_ref, *refs, t, npages, past):
    pages = refs[:P_PAGES]
    (qaug_ref, new_ref, oh_ref, winkv_ref, gates_ref, ocmp_ref, out_ref, m_ref, acc_ref) = refs[P_PAGES:]
    step = pl.program_id(1)
    nsteps = npages // P_PAGES
    tk = P_PAGES * PAGE
    lo_rows = lax.broadcasted_iota(jnp.int32, (LANES, 1), 0) < HEAD_DIM
    one = jnp.ones((), BF16)

    def update(s, vt, kposrel, qrel):
        for kv in range(KV_NSA):
            vaug = jnp.where(lo_rows, vt, one) if kv == 0 else jnp.where(lo_rows, one, vt)
            _flash_update(s[kv * GQA * t:(kv + 1) * GQA * t], vaug, kposrel, qrel, m_ref, acc_ref, t, kv, True)

    @pl.when(step == 0)
    def _():
        m_ref[...] = jnp.full(m_ref.shape, NEG, F32)
        acc_ref[...] = jnp.zeros(acc_ref.shape, F32)
        s = jnp.dot(qaug_ref[:, 0:LANES], new_ref[0:LANES, :], preferred_element_type=F32)
        j = lax.broadcasted_iota(jnp.int32, (1, LANES), 1)
        s = jnp.where(j < t, s, NEG)
        update(s, new_ref[LANES:2 * LANES, :], j, lax.broadcasted_iota(jnp.int32, (t, 1), 0))

    pg0 = npages - (step + 1) * P_PAGES
    k0 = pl.multiple_of(pg0 * PAGE, tk)
    kt = jnp.concatenate([r[0:LANES, :].astype(BF16) for r in pages], axis=1)
    vt = jnp.concatenate([r[LANES:2 * LANES, :].astype(BF16) for r in pages], axis=1)
    kaug = jnp.concatenate([kt, oh_ref[:, pl.ds(k0, tk)]], axis=0)
    s = jnp.dot(qaug_ref[...], kaug, preferred_element_type=F32)
    kposrel = (k0 - past) + lax.broadcasted_iota(jnp.int32, (1, tk), 1)
    update(s, vt, kposrel, None)

    @pl.when(step == nsteps - 1)
    def _():
        o_sel = _normalised(acc_ref)
        valid_fn = lambda j, r: (j >= r) & (j <= r + WINDOW) & (j < WINDOW + t)
        o_win = _window_attn(qaug_ref[:, 0:LANES], winkv_ref[:, 0:LANES], winkv_ref[:, LANES:2 * LANES],
                             valid_fn, t)
        _combine(o_sel, o_win, ocmp_ref[...], gates_ref[...], out_ref, t)


def _nsa_sample(cache_t, page_table, qaug, selnew_t, onehot_t, winkv, gates, ocmp, *, past):
    nseq, npages = page_table.shape
    t = gates.shape[1]
    rows = N_SLAB * t
    tk = P_PAGES * PAGE

    def page_map(i):
        return lambda s, j, pt: (pt[s * npages + npages - (j + 1) * P_PAGES + i], 1, 0)

    seq = lambda s, j, pt: (s, 0, 0)
    grid_spec = pltpu.PrefetchScalarGridSpec(
        num_scalar_prefetch=1,
        grid=(nseq, npages // P_PAGES),
        in_specs=[pl.BlockSpec((None, 2 * LANES, PAGE), page_map(i)) for i in range(P_PAGES)] + [
            pl.BlockSpec((None, rows, 2 * LANES), seq),
            pl.BlockSpec((None, 2 * LANES, LANES), seq),
            pl.BlockSpec(onehot_t.shape, lambda s, j, pt: (0, 0)),
            pl.BlockSpec((None, winkv.shape[1], 2 * LANES), seq),
            pl.BlockSpec((None, t, LANES), seq),
            pl.BlockSpec((None, t, 4 * LANES), seq),
        ],
        out_specs=pl.BlockSpec((None, t, 4 * LANES), seq),
        scratch_shapes=[
            pltpu.VMEM((rows, LANES), F32),
            pltpu.VMEM((rows, LANES), F32),
        ],
    )
    return pl.pallas_call(
        functools.partial(_nsa_sample_body, t=t, npages=npages, past=past),
        out_shape=jax.ShapeDtypeStruct((nseq, t, 4 * LANES), F32),
        grid_spec=grid_spec,
        compiler_params=_cparams(("parallel", "arbitrary")),
        name="nsa_sample",
    )(page_table.reshape(-1), *([cache_t] * P_PAGES), qaug, selnew_t, onehot_t, winkv, gates, ocmp)


def _sb_sample_body(pt_ref, *refs, t, nh):
    pages = refs[:P_PAGES]
    qbd_ref, new_ref, uj_ref, out_ref, r_ref, acc_ref = refs[P_PAGES:]
    step = pl.program_id(1)
    nsteps = pl.num_programs(1)
    d = nh * HEAD_DIM
    qbd = qbd_ref[...]
    uj = uj_ref[...]

    def chunk(z, r, before):
        sp = _softplus2(z)
        nl = sp if before is None else jnp.where(before, sp, 0.0)
        w = _split_dot(nl, uj)
        a = jnp.exp2((z - sp) - (r + w[:, 0:LANES]))
        if before is not None:
            a = jnp.where(before, a, 0.0)
        return a.astype(BF16), r + w[:, LANES:2 * LANES]

    @pl.when(step == 0)
    def _():
        rows = lax.broadcasted_iota(jnp.int32, (nh * t, LANES), 0)
        j = lax.broadcasted_iota(jnp.int32, (nh * t, LANES), 1)
        before = j < (rows & (t - 1))
        z = jnp.dot(qbd, new_ref[0:d, :], preferred_element_type=F32)
        a, r = chunk(z, jnp.zeros((nh * t, LANES), F32), before)
        r_ref[...] = r
        acc_ref[...] = lax.dot_general(a, new_ref[d:2 * d, :], NT_DIMS, preferred_element_type=F32)

    kt = jnp.concatenate([x[0:d, :].astype(BF16) for x in pages], axis=1)
    vt = jnp.concatenate([x[d:2 * d, :].astype(BF16) for x in pages], axis=1)
    z = jnp.dot(qbd, kt, preferred_element_type=F32)
    r = r_ref[...]
    parts = [None] * P_PAGES
    for i in reversed(range(P_PAGES)):
        parts[i], r = chunk(z[:, i * PAGE:(i + 1) * PAGE], r, None)
    r_ref[...] = r
    acc_ref[...] += lax.dot_general(jnp.concatenate(parts, axis=1), vt, NT_DIMS, preferred_element_type=F32)

    @pl.when(step == nsteps - 1)
    def _():
        col = jnp.right_shift(lax.broadcasted_iota(jnp.int32, (t, d), 1), 6)
        o = jnp.zeros((t, d), F32)
        for h in range(nh):
            o = o + jnp.where(col == h, acc_ref[h * t:(h + 1) * t, :], 0.0)
        out_ref[...] = o


def _sb_sample(cache_t, page_table, qbd, sbnew_t, uj, *, t):
    nseq, npages = page_table.shape
    d2 = cache_t.shape[1]
    d = d2 // 2
    nh = d // HEAD_DIM

    def page_map(i):
        return lambda s, j, pt: (pt[s * npages + npages - (j + 1) * P_PAGES + i], 0, 0)

    seq = lambda s, j, pt: (s, 0, 0)
    grid_spec = pltpu.PrefetchScalarGridSpec(
        num_scalar_prefetch=1,
        grid=(nseq, npages // P_PAGES),
        in_specs=[pl.BlockSpec((None, d2, PAGE), page_map(i)) for i in range(P_PAGES)] + [
            pl.BlockSpec((None, nh * t, d), seq),
            pl.BlockSpec((None, d2, LANES), seq),
            pl.BlockSpec(uj.shape, lambda s, j, pt: (0, 0)),
        ],
        out_specs=pl.BlockSpec((None, t, d), seq),
        scratch_shapes=[pltpu.VMEM((nh * t, LANES), F32), pltpu.VMEM((nh * t, d), F32)],
    )
    return pl.pallas_call(
        functools.partial(_sb_sample_body, t=t, nh=nh),
        out_shape=jax.ShapeDtypeStruct((nseq, t, d), F32),
        grid_spec=grid_spec,
        compiler_params=_cparams(("parallel", "arbitrary")),
        name="sb_sample",
    )(page_table.reshape(-1), *([cache_t] * P_PAGES), qbd, sbnew_t, uj)


def _outproj_body(x_ref, oa_ref, ob_ref, ga_ref, gb_ref, wa_ref, wb_ref, o_ref):
    a = _rms(oa_ref[...], ga_ref[...]).astype(BF16)
    b = _rms(ob_ref[...], gb_ref[...]).astype(BF16)
    o_ref[...] = (x_ref[...] + jnp.dot(a, wa_ref[...], preferred_element_type=F32)
                  + jnp.dot(b, wb_ref[...], preferred_element_type=F32))


def _outproj(x, oa, ob, ga, gb, wa, wb):
    n, d = x.shape
    da, db = oa.shape[1], ob.shape[1]
    tm = min(TM_FFN, n)
    const = lambda i: (0, 0)
    row = lambda i: (i, 0)
    return pl.pallas_call(
        _outproj_body,
        out_shape=jax.ShapeDtypeStruct((n, d), F32),
        grid=(n // tm,),
        in_specs=[
            pl.BlockSpec((tm, d), row), pl.BlockSpec((tm, da), row), pl.BlockSpec((tm, db), row),
            pl.BlockSpec((1, da), const), pl.BlockSpec((1, db), const),
            pl.BlockSpec((da, d), const), pl.BlockSpec((db, d), const),
        ],
        out_specs=pl.BlockSpec((tm, d), row),
        compiler_params=_cparams(("parallel",)),
        name="outproj",
    )(x, oa, ob, ga, gb, wa, wb)


def _qa_perm():
    idx = np.zeros((KV_NSA * GQA * HEAD_DIM,), np.int32)
    for kv in range(KV_NSA):
        for g in range(GQA):
            for d in range(HEAD_DIM):
                idx[LANES * g + HEAD_DIM * kv + d] = (kv * GQA + g) * HEAD_DIM + d
    return idx


def _pages_feature_major(cache):
    n_pool = cache.shape[0]
    return jnp.transpose(cache, (0, 2, 3, 4, 1)).reshape(n_pool, -1, PAGE)


def _layer(xp, xs, cache_nsa, state_win, cache_sb, page_table, p):
    nb, s, d = xp.shape
    db, t, _ = xs.shape
    npages = page_table.shape[1]
    past = npages * PAGE
    d_nsa = KV_NSA * GQA * HEAD_DIM
    d_sb = p['out_norm_b'].shape[0]
    nh = d_sb // HEAD_DIM
    assert s % TK_NSA == 0 and s % TQ_SB == 0 and s // SEL_BLK <= LANES and s >= WINDOW
    assert past // SEL_BLK <= LANES and t & (t - 1) == 0 and t <= SEL_BLK and state_win.shape[1] == WINDOW
    assert npages % P_PAGES == 0 and npages % P_CMP == 0 and db % SEQ_CMP == 0 and cache_nsa.shape[1] == PAGE

    perm = _qa_perm()
    o1, o2, o3, o4 = 512, 1024, 1280, 1304
    cols = np.concatenate([perm, np.arange(o1, o3), np.arange(o4, o4 + 3 * d_sb), np.arange(o3, o4)])
    w_packed = jnp.pad(p['w_in'][:, cols], ((0, 0), (0, LANES - (o4 - o3)))).astype(BF16)
    tile2 = lambda v: jnp.tile(v, 2).reshape(1, LANES)
    qn, kn_cmp, kn_sel, kn_win = tile2(p['q_norm']), tile2(p['k_norm'][0]), tile2(p['k_norm'][1]), tile2(p['k_norm'][2])
    gidx = np.arange(LANES) // HEAD_DIM
    gsum = jnp.asarray(gidx[:, None] == gidx[None, :], BF16)
    w_sm = jax.nn.softmax(p['cmp_pos_w'].astype(F32), axis=0)
    w_cmp = jnp.tile(jnp.repeat(w_sm, HEAD_DIM, axis=1), (1, 2))
    w_cmp_t = jnp.tile(w_cmp.T, (1, PAGE // CMP_BLK))
    ga = p['out_norm_a'][perm].reshape(1, d_nsa)
    gb = p['out_norm_b'].reshape(1, d_sb)
    wo_a = p['w_o'][:d_nsa][perm].astype(BF16)
    wo_b = p['w_o'][d_nsa:].astype(BF16)
    wgu1, wd1 = p['w_ffn1_gu'].astype(BF16), p['w_ffn1_down'].astype(BF16)
    wgu2, wd2 = p['w_ffn2_gu'].astype(BF16), p['w_ffn2_down'].astype(BF16)
    g1, gm, g2 = (p[k].reshape(1, d) for k in ('norm_ffn1', 'norm_mix', 'norm_ffn2'))

    def pair_mat(n_cmp):
        return jnp.asarray(np.arange(n_cmp)[:, None] // 2 == np.arange(LANES)[None, :], BF16)

    xp2 = xp.reshape(nb * s, d)
    xs2 = xs.reshape(db * t, d)

    xp2 = _ffn(xp2, g1, wgu1, wd1)
    xs2 = _ffn(xs2, g1, wgu1, wd1)
    proj_p = _inproj(xp2, gm, w_packed, qn, kn_sel, kn_win, gsum)
    proj_s = _inproj(xs2, gm, w_packed, qn, kn_sel, kn_win, gsum)
    qa_p, nsa_p, selkv_p, win_p, winb_p, qb_p, sb_p, sbb_p, gates_p = (a.reshape(nb, s, -1) for a in proj_p)
    qa_s, nsa_s, selkv_s, win_s, winb_s, qb_s, sb_s, sbb_s, gates_s = (a.reshape(db, t, -1) for a in proj_s)

    kcvc_p = _compress_rows(nsa_p, w_cmp, kn_cmp, gsum)
    ocmp_p, selb_p = _cmp_topk(qa_p, kcvc_p, pair_mat(s // CMP_BLK), nsb=1, tq=TQ_CMP, pos_base=0, k_lanes=TOP_N)
    nqb, nkt = s // TQ_NSA, s // TK_NSA
    picked = jnp.any((selb_p == 0).reshape(nb, KV_NSA, nqb, TQ_NSA, LANES), axis=3)[..., :s // SEL_BLK]
    tile_id = np.arange(nkt)
    earlier = jnp.asarray(tile_id[None, :] < (np.arange(nqb) * TQ_NSA // TK_NSA)[:, None])
    active = jnp.any(picked.reshape(nb, KV_NSA, nqb, nkt, TK_NSA // SEL_BLK), axis=-1) & earlier
    active = active.transpose(0, 2, 1, 3)
    lists = jnp.sort(jnp.where(active, tile_id, nkt), axis=-1).astype(jnp.int32).reshape(-1)
    counts = jnp.sum(active, axis=-1).astype(jnp.int32).reshape(-1)
    onehot = jnp.asarray(np.arange(s)[:, None] // SEL_BLK == np.arange(LANES)[None, :], BF16)
    winkv_pad = jnp.pad(winb_p, ((0, 0), (WINDOW, 0), (0, 0)))
    onsa_p = _nsa_prompt(counts, lists, qa_p, gates_p, ocmp_p, selb_p, selkv_p, winkv_pad, onehot)

    umat_p = jnp.asarray(np.arange(TQ_SB)[:, None] > np.arange(TQ_SB)[None, :], BF16)
    osb_p = _sb_prompt(qb_p, sbb_p, umat_p)

    cache_nsa_t = _pages_feature_major(cache_nsa)
    blk = np.arange(PAGE) // CMP_BLK
    sel_rows = np.zeros((2, 16, PAGE), np.float32)
    for h in range(2):
        sel_rows[h, 4 * h + blk, np.arange(PAGE)] = 1.0
    kcvc_s = _compress_pages(cache_nsa_t, page_table, w_cmp_t, jnp.asarray(sel_rows, BF16), kn_cmp, gsum)
    k_lanes = TOP_N - 1
    ocmp_s, selb_s = _cmp_topk(qa_s, kcvc_s, pair_mat(past // CMP_BLK), nsb=SEQ_CMP, tq=t, pos_base=past,
                               k_lanes=k_lanes)
    lane_lo = (np.arange(LANES) < HEAD_DIM)
    halfmask = jnp.asarray(np.stack([lane_lo, ~lane_lo]), BF16)
    q4 = qa_s.reshape(db, t, GQA, LANES).transpose(0, 2, 1, 3)
    qrows = q4[:, None] * halfmask[None, :, None, None, :]
    brows = jnp.broadcast_to(selb_s[:, :, None], (db, KV_NSA, GQA, t, LANES))
    qaug_s = jnp.concatenate([qrows, brows], axis=-1).reshape(db, N_SLAB * t, 2 * LANES)
    selnew_t = jnp.pad(selkv_s.transpose(0, 2, 1), ((0, 0), (0, 0), (0, LANES - t)))
    onehot_t = jnp.asarray(np.arange(LANES)[:, None] == np.arange(past)[None, :] // SEL_BLK, BF16)
    win_all = jnp.concatenate([state_win.reshape(db, WINDOW, 2 * LANES), win_s], axis=1)
    winkv_s = jnp.pad(win_all.astype(BF16), ((0, 0), (0, LANES - t), (0, 0)))
    onsa_s = _nsa_sample(cache_nsa_t, page_table, qaug_s, selnew_t, onehot_t, winkv_s, gates_s, ocmp_s, past=past)

    cache_sb_t = _pages_feature_major(cache_sb)
    eye = jnp.eye(nh, dtype=BF16)
    qbd = jnp.einsum('bthd,hg->bhtgd', qb_s.reshape(db, t, nh, HEAD_DIM), eye).reshape(db, nh * t, d_sb)
    sbnew_t = jnp.pad(sbb_s.transpose(0, 2, 1), ((0, 0), (0, 0), (0, LANES - t)))
    jj = np.arange(PAGE)
    uj = jnp.asarray(np.concatenate([jj[:, None] > jj[None, :], np.ones((PAGE, PAGE), bool)], axis=1), BF16)
    osb_s = _sb_sample(cache_sb_t, page_table, qbd, sbnew_t, uj, t=t)

    xp2 = _outproj(xp2, onsa_p.reshape(nb * s, d_nsa), osb_p.reshape(nb * s, d_sb), ga, gb, wo_a, wo_b)
    xs2 = _outproj(xs2, onsa_s.reshape(db * t, d_nsa), osb_s.reshape(db * t, d_sb), ga, gb, wo_a, wo_b)
    xp2 = _ffn(xp2, g2, wgu2, wd2)
    xs2 = _ffn(xs2, g2, wgu2, wd2)

    wb = min(WINDOW, s)
    outs = (nsa_p.reshape(nb, s, 4, KV_NSA, HEAD_DIM),
            nsa_s.reshape(db, t, 4, KV_NSA, HEAD_DIM),
            win_p[:, s - wb:].reshape(nb, wb, 2, KV_NSA, HEAD_DIM),
            win_all[:, t:].reshape(db, WINDOW, 2, KV_NSA, HEAD_DIM),
            sb_p.reshape(nb, s, 2, nh, HEAD_DIM),
            sb_s.reshape(db, t, 2, nh, HEAD_DIM))
    return xp2.reshape(nb, s, d), xs2.reshape(db, t, d), outs


def kernel(x_prompt, x_sample, cache_nsa, state_win, cache_sb, page_table, norm_ffn1, w_ffn1_gu, w_ffn1_down,
           norm_mix, w_in, q_norm, k_norm, cmp_pos_w, out_norm_a, out_norm_b, w_o, norm_ffn2, w_ffn2_gu,
           w_ffn2_down):
    names = ('norm_ffn1', 'w_ffn1_gu', 'w_ffn1_down', 'norm_mix', 'w_in', 'q_norm', 'k_norm', 'cmp_pos_w',
             'out_norm_a', 'out_norm_b', 'w_o', 'norm_ffn2', 'w_ffn2_gu', 'w_ffn2_down')
    params = (norm_ffn1, w_ffn1_gu, w_ffn1_down, norm_mix, w_in, q_norm, k_norm, cmp_pos_w,
              out_norm_a, out_norm_b, w_o, norm_ffn2, w_ffn2_gu, w_ffn2_down)
    depth = w_in.shape[0]
    xp, xs = x_prompt, x_sample
    per_layer = []
    for l in range(depth):
        p = {k: v[l] for k, v in zip(names, params)}
        xp, xs, outs = _layer(xp, xs, cache_nsa[l], state_win[l], cache_sb[l], page_table, p)
        per_layer.append(outs)
    stacked = tuple(jnp.stack([o[i] for o in per_layer]) for i in range(6))
    return (xp, xs) + stacked
```

```python
import functools

import numpy as np
import jax
import jax.numpy as jnp
from jax import lax
from jax.experimental import pallas as pl
from jax.experimental.pallas import tpu as pltpu

F32 = jnp.float32
BF16 = jnp.bfloat16

HEAD_DIM = 64
KV_NSA = 2
GQA = 4
N_SLAB = KV_NSA * GQA
CMP_BLK = 32
SEL_BLK = 64
TOP_N = 16
WINDOW = 512
EPS = 1e-6
NEG = -1e30
FORCED = 1e6
SCALE = HEAD_DIM ** -0.5
LOG2E = 1.4426950408889634
LANES = 128
PAGE = 128

VMEM_LIMIT = 56 * 1024 * 1024

TM_FFN = 512
FF_CHUNK = 256
TQ_CMP = 128
SEQ_CMP = 8
TQ_NSA = 128
TK_NSA = 256
TQ_SB = 256
RS_SB = 128
UNROLL_NSA = 4
UNROLL_SB = 8
P_PAGES = 16
P_CMP = 32

NT_DIMS = (((1,), (1,)), ((), ()))


def _cparams(sem):
    return pltpu.CompilerParams(dimension_semantics=sem, vmem_limit_bytes=VMEM_LIMIT)


def _rms(x, w):
    ms = jnp.mean(x * x, axis=-1, keepdims=True)
    return x * lax.rsqrt(ms + EPS) * w


def _split(x):
    hi = x.astype(BF16)
    return hi, (x - hi.astype(F32)).astype(BF16)


def _split_dot(x, m):
    hi, lo = _split(x)
    return jnp.dot(hi, m, preferred_element_type=F32) + jnp.dot(lo, m, preferred_element_type=F32)


def _group_rms(x, gsum, w):
    ms = _split_dot(x * x, gsum) * (1.0 / HEAD_DIM)
    return x * lax.rsqrt(ms + EPS) * w


def _softplus2(z):
    neg_abs = lax.bitcast_convert_type(lax.bitcast_convert_type(z, jnp.uint32) | jnp.uint32(0x80000000), F32)
    return jnp.maximum(z, 0.0) + jnp.log2(1.0 + jnp.exp2(neg_abs))


def _unrolled_loop(n, fn, unroll):
    def body(i, carry):
        for u in range(unroll):
            fn(unroll * i + u)
        return carry

    lax.fori_loop(0, n // unroll, body, 0)
    base = (n // unroll) * unroll
    rem = n - base
    w = unroll // 2
    while w >= 1:
        off = base + (rem & ~(2 * w - 1))

        @pl.when((rem & w) != 0)
        def _(off=off, w=w):
            for u in range(w):
                fn(off + u)
        w //= 2


def _ffn_body(x_ref, g_ref, wgu_ref, wd_ref, o_ref, acc_ref, *, d_ff):
    x = x_ref[...]
    h = _rms(x, g_ref[...]).astype(BF16)
    for c in range(d_ff // FF_CHUNK):
        lo = c * FF_CHUNK
        g = jnp.dot(h, wgu_ref[:, lo:lo + FF_CHUNK], preferred_element_type=F32)
        u = jnp.dot(h, wgu_ref[:, d_ff + lo:d_ff + lo + FF_CHUNK], preferred_element_type=F32)
        a = (g / (1.0 + jnp.exp(-g)) * u).astype(BF16)
        d = jnp.dot(a, wd_ref[lo:lo + FF_CHUNK, :], preferred_element_type=F32)
        if c == 0:
            acc_ref[...] = d
        else:
            acc_ref[...] += d
    o_ref[...] = x + 0.5 * acc_ref[...]


def _ffn(x, gain, wgu, wd):
    n, d = x.shape
    d_ff = wd.shape[0]
    tm = min(TM_FFN, n)
    const = lambda i: (0, 0)
    return pl.pallas_call(
        functools.partial(_ffn_body, d_ff=d_ff),
        out_shape=jax.ShapeDtypeStruct((n, d), F32),
        grid=(n // tm,),
        in_specs=[
            pl.BlockSpec((tm, d), lambda i: (i, 0)),
            pl.BlockSpec((1, d), const),
            pl.BlockSpec((d, 2 * d_ff), const, pipeline_mode=pl.Buffered(1)),
            pl.BlockSpec((d_ff, d), const, pipeline_mode=pl.Buffered(1)),
        ],
        out_specs=pl.BlockSpec((tm, d), lambda i: (i, 0)),
        scratch_shapes=[pltpu.VMEM((tm, d), F32)],
        compiler_params=_cparams(("parallel",)),
        name="ffn",
    )(x, gain, wgu, wd)


def _inproj_body(x_ref, g_ref, w_ref, qn_ref, ksel_ref, kwin_ref, gsum_ref,
                 qa_ref, nsa_ref, selkv_ref, win_ref, winb_ref, qb_ref, sb_ref, sbb_ref, gates_ref, *, feature_major):
    def put(ref, lo, val):
        for c in range(val.shape[1] // LANES):
            piece = val[:, c * LANES:(c + 1) * LANES]
            if feature_major:
                ref[lo + c * LANES:lo + (c + 1) * LANES, :] = piece.T
            else:
                ref[:, lo + c * LANES:lo + (c + 1) * LANES] = piece

    h = _rms(x_ref[...], g_ref[...]).astype(BF16)
    gsum = gsum_ref[...]
    dot = lambda lo, hi: jnp.dot(h, w_ref[:, lo:hi], preferred_element_type=F32)

    qn = qn_ref[...]
    for j in range(4):
        pq = dot(j * LANES, (j + 1) * LANES)
        qa_ref[:, j * LANES:(j + 1) * LANES] = (_group_rms(pq, gsum, qn) * (SCALE * LOG2E)).astype(BF16)

    pn = dot(512, 1024)
    selk = _group_rms(pn[:, 256:384], gsum, ksel_ref[...])
    put(nsa_ref, 0, pn[:, 0:256])
    put(nsa_ref, 256, selk)
    put(nsa_ref, 384, pn[:, 384:512])
    selkv_ref[:, 0:128] = selk.astype(BF16)
    selkv_ref[:, 128:256] = pn[:, 384:512].astype(BF16)

    pw = dot(1024, 1280)
    wk = _group_rms(pw[:, 0:128], gsum, kwin_ref[...])
    win_ref[:, 0:128] = wk
    win_ref[:, 128:256] = pw[:, 128:256]
    winb_ref[:, 0:128] = wk.astype(BF16)
    winb_ref[:, 128:256] = pw[:, 128:256].astype(BF16)

    qb_ref[...] = (dot(1280, 1792) * (SCALE * LOG2E)).astype(BF16)
    ps = dot(1792, 2816)
    put(sb_ref, 0, ps)
    sbb_ref[...] = ps.astype(BF16)
    pg = dot(2816, 2944)
    gates_ref[...] = 1.0 / (1.0 + jnp.exp(-pg))


def _inproj(x, gain, w_packed, qn, ksel, kwin, gsum, *, seq_len=None):
    n, d = x.shape
    tm = min(TM_FFN, n)
    const = lambda i: (0, 0)
    row = lambda i: (i, 0)
    widths = [(512, BF16), (512, F32), (256, BF16), (256, F32), (256, BF16), (512, BF16),
              (1024, F32), (1024, BF16), (128, F32)]
    out_shape = [jax.ShapeDtypeStruct((n, w), dt) for w, dt in widths]
    out_specs = [pl.BlockSpec((tm, w), row) for w, _ in widths]
    if seq_len is not None:
        spt = seq_len // tm
        for k in (1, 6):
            w = widths[k][0]
            out_shape[k] = jax.ShapeDtypeStruct((n // seq_len, w, seq_len), F32)
            out_specs[k] = pl.BlockSpec((None, w, tm), lambda i: (i // spt, 0, i % spt))
    return pl.pallas_call(
        functools.partial(_inproj_body, feature_major=seq_len is not None),
        out_shape=out_shape,
        grid=(n // tm,),
        in_specs=[
            pl.BlockSpec((tm, d), row),
            pl.BlockSpec((1, d), const),
            pl.BlockSpec(w_packed.shape, const, pipeline_mode=pl.Buffered(1)),
            pl.BlockSpec((1, LANES), const),
            pl.BlockSpec((1, LANES), const),
            pl.BlockSpec((1, LANES), const),
            pl.BlockSpec((LANES, LANES), const),
        ],
        out_specs=out_specs,
        compiler_params=_cparams(("parallel",)),
        name="inproj",
    )(x, gain, w_packed, qn, ksel, kwin, gsum)


def _compress_pages_body(pt_ref, *refs):
    pages = refs[:P_CMP]
    wt_ref, sel_ref, kn_ref, gsum_ref, out_ref = refs[P_CMP:]
    wt = wt_ref[...]
    pieces = []
    for k in range(P_CMP // 2):
        acc = None
        for h in range(2):
            hi, lo = _split(pages[2 * k + h][...] * wt)
            r = (lax.dot_general(sel_ref[h], hi, NT_DIMS, preferred_element_type=F32)
                 + lax.dot_general(sel_ref[h], lo, NT_DIMS, preferred_element_type=F32))
            acc = r if acc is None else acc + r
        pieces.append(acc[0:8])
    c = jnp.concatenate(pieces, axis=0)
    kc = _group_rms(c[:, 0:LANES], gsum_ref[...], kn_ref[...])
    out_ref[:, 0:LANES] = kc.astype(BF16)
    out_ref[:, LANES:2 * LANES] = c[:, LANES:2 * LANES].astype(BF16)


def _compress_pages(cache_t, page_table, wt, sel, kn, gsum, *, contiguous=False):
    nseq, npages = page_table.shape
    rows = P_CMP * PAGE // CMP_BLK

    def page_map(i):
        if contiguous:
            return lambda s, j, pt: (s, 0, j * P_CMP + i)
        return lambda s, j, pt: (pt[s * npages + j * P_CMP + i], 0, 0)

    const2 = lambda s, j, pt: (0, 0)
    grid_spec = pltpu.PrefetchScalarGridSpec(
        num_scalar_prefetch=1,
        grid=(nseq, npages // P_CMP),
        in_specs=[pl.BlockSpec((None, 2 * LANES, PAGE), page_map(i)) for i in range(P_CMP)] + [
            pl.BlockSpec((2 * LANES, PAGE), const2),
            pl.BlockSpec(sel.shape, lambda s, j, pt: (0, 0, 0)),
            pl.BlockSpec((1, LANES), const2),
            pl.BlockSpec((LANES, LANES), const2),
        ],
        out_specs=pl.BlockSpec((None, rows, 2 * LANES), lambda s, j, pt: (s, j, 0)),
    )
    return pl.pallas_call(
        _compress_pages_body,
        out_shape=jax.ShapeDtypeStruct((nseq, npages * PAGE // CMP_BLK, 2 * LANES), BF16),
        grid_spec=grid_spec,
        compiler_params=_cparams(("parallel", "arbitrary")),
        name="compress_pages",
    )(page_table.reshape(-1), *([cache_t] * P_CMP), wt, sel, kn, gsum)


def _cmp_topk_body(q_ref, kcvc_ref, pair_ref, ocmp_ref, selb_ref, *, nsb, tq, pos_base, k_lanes):
    n_cmp = kcvc_ref.shape[1]
    nr = nsb * tq
    qpos1 = pos_base + pl.program_id(1) * tq + lax.broadcasted_iota(jnp.int32, (tq, 1), 0)
    qpos = jnp.concatenate([qpos1] * nsb, axis=0)
    end = (lax.broadcasted_iota(jnp.int32, (1, n_cmp), 1) + 1) * CMP_BLK - 1
    dist = qpos - end
    valid = dist >= 0
    distf = dist.astype(F32)
    lane = lax.broadcasted_iota(jnp.int32, (1, LANES), 1)
    lo_half = lane < HEAD_DIM
    lanef = lane.astype(F32)
    n_blk = n_cmp * CMP_BLK // SEL_BLK
    cur = jnp.right_shift(qpos, 6)
    forced = (lane == 0) | (lane == cur) | (lane == cur - 1)
    avail = (lane * SEL_BLK <= qpos) & (lane < n_blk)
    seq_rows = [slice(sq * tq, (sq + 1) * tq) for sq in range(nsb)]

    outs = [[None] * KV_NSA for _ in range(GQA)]
    imps = []
    for kv in range(KV_NSA):
        half = lo_half if kv == 0 else jnp.logical_not(lo_half)
        imp = jnp.zeros((nr, n_cmp), F32)
        for g in range(GQA):
            s = jnp.concatenate([
                lax.dot_general(jnp.where(half, q_ref[sq, :, g * LANES:(g + 1) * LANES], jnp.zeros((), BF16)),
                                kcvc_ref[sq, :, 0:LANES], NT_DIMS, preferred_element_type=F32)
                for sq in range(nsb)], axis=0)
            s = s - (LOG2E * 2.0 ** -(kv * GQA + g + 1)) * distf
            s = jnp.where(valid, s, NEG)
            m = jnp.max(s, axis=-1, keepdims=True)
            e = jnp.where(valid, jnp.exp2(s - m), 0.0)
            p = e / jnp.maximum(jnp.sum(e, axis=-1, keepdims=True), 1e-30)
            outs[g][kv] = [jnp.dot(p[seq_rows[sq]].astype(BF16), kcvc_ref[sq, :, LANES:2 * LANES],
                                   preferred_element_type=F32) for sq in range(nsb)]
            imp = imp + p
        imps.append(imp)
    imp = jnp.concatenate(imps, axis=0)
    avail2 = jnp.concatenate([avail] * KV_NSA, axis=0)
    forced2 = jnp.concatenate([forced] * KV_NSA, axis=0)
    score = jnp.where(avail2, jnp.where(forced2, FORCED, _split_dot(imp, pair_ref[...])), NEG)
    sel = jnp.zeros((KV_NSA * nr, LANES), F32)
    for _ in range(k_lanes):
        mx = jnp.max(score, axis=-1, keepdims=True)
        idx = jnp.min(jnp.where(score == mx, lanef, 1e9), axis=-1, keepdims=True)
        hit = lanef == idx
        sel = jnp.where(hit, 1.0, sel)
        score = jnp.where(hit, -3e38, score)
    selb = jnp.where(sel > 0.5, 0.0, NEG)
    for sq in range(nsb):
        for kv in range(KV_NSA):
            selb_ref[sq, kv] = selb[kv * nr + sq * tq:kv * nr + (sq + 1) * tq].astype(BF16)
        for g in range(GQA):
            ocmp_ref[sq, :, g * LANES:(g + 1) * LANES] = jnp.where(lo_half, outs[g][0][sq], outs[g][1][sq])


def _cmp_topk(q_a, kcvc, pair, *, nsb, tq, pos_base, k_lanes):
    nseq, t, _ = q_a.shape
    n_cmp = kcvc.shape[1]
    return pl.pallas_call(
        functools.partial(_cmp_topk_body, nsb=nsb, tq=tq, pos_base=pos_base, k_lanes=k_lanes),
        out_shape=[jax.ShapeDtypeStruct((nseq, t, 4 * LANES), F32),
                   jax.ShapeDtypeStruct((nseq, KV_NSA, t, LANES), BF16)],
        grid=(nseq // nsb, t // tq),
        in_specs=[
            pl.BlockSpec((nsb, tq, 4 * LANES), lambda b, i: (b, i, 0)),
            pl.BlockSpec((nsb, n_cmp, 2 * LANES), lambda b, i: (b, 0, 0)),
            pl.BlockSpec(pair.shape, lambda b, i: (0, 0)),
        ],
        out_specs=[pl.BlockSpec((nsb, tq, 4 * LANES), lambda b, i: (b, i, 0)),
                   pl.BlockSpec((nsb, KV_NSA, tq, LANES), lambda b, i: (b, 0, i, 0))],
        compiler_params=_cparams(("parallel", "parallel")),
        name="cmp_topk",
    )(q_a, kcvc, pair)


def _build_qaug(q, selb_ref, qaug_ref, tq):
    lane = lax.broadcasted_iota(jnp.int32, (1, LANES), 1)
    lo_half = lane < HEAD_DIM
    for kv in range(KV_NSA):
        half = lo_half if kv == 0 else jnp.logical_not(lo_half)
        for g in range(GQA):
            r0 = (kv * GQA + g) * tq
            qaug_ref[r0:r0 + tq, 0:LANES] = jnp.where(half, q[:, g * LANES:(g + 1) * LANES],
                                                      jnp.zeros((), BF16))
            qaug_ref[r0:r0 + tq, LANES:2 * LANES] = selb_ref[kv]


def _flash_update(s, vaug, kposrel, qrel, m_ref, acc_ref, tq, kv, v_pos_minor):
    kf = kposrel.astype(F32)
    tk = s.shape[1]
    rows = slice(kv * GQA * tq, (kv + 1) * GQA * tq)
    m_prev = m_ref[rows]
    m_parts, p_parts = [], []
    for j in range(GQA):
        sub = slice(j * tq, (j + 1) * tq)
        si = s[sub] + (LOG2E * 2.0 ** -(kv * GQA + j + 1)) * kf
        if qrel is not None:
            si = jnp.where(kposrel <= qrel, si, NEG)
        m_new = jnp.maximum(m_prev[sub], jnp.max(si, axis=-1, keepdims=True))
        m_parts.append(m_new)
        p_parts.append(jnp.concatenate(
            [jnp.exp2(si[:, c * LANES:(c + 1) * LANES] - m_new) for c in range(tk // LANES)], axis=1))
    m_new = jnp.concatenate(m_parts, axis=0)
    p = jnp.concatenate(p_parts, axis=0).astype(BF16)
    if v_pos_minor:
        pv = lax.dot_general(p, vaug, NT_DIMS, preferred_element_type=F32)
    else:
        pv = jnp.dot(p, vaug, preferred_element_type=F32)
    acc_ref[rows] = jnp.exp2(m_prev - m_new) * acc_ref[rows] + pv
    m_ref[rows] = m_new


def _normalised(acc_ref):
    acc = acc_ref[...]
    return acc / pltpu.roll(acc, HEAD_DIM, 1)


def _window_attn(qw, kw, vw, valid_fn, tq):
    s = lax.dot_general(qw, kw, NT_DIMS, preferred_element_type=F32)
    nk = kw.shape[0]
    j = lax.broadcasted_iota(jnp.int32, (1, nk), 1)
    r = lax.broadcasted_iota(jnp.int32, (tq, 1), 0)
    valid = valid_fn(j, r)
    jf = j.astype(F32)
    ps, ls = [], []
    for i in range(N_SLAB):
        si = s[i * tq:(i + 1) * tq] + (LOG2E * 2.0 ** -(i + 1)) * jf
        si = jnp.where(valid, si, NEG)
        m = jnp.max(si, axis=-1, keepdims=True)
        e = jnp.where(valid, jnp.exp2(si - m), 0.0)
        ls.append(jnp.maximum(jnp.sum(e, axis=-1, keepdims=True), 1e-30))
        ps.append(e.astype(BF16))
    o = jnp.dot(jnp.concatenate(ps, axis=0), vw, preferred_element_type=F32)
    return o / jnp.concatenate(ls, axis=0)


def _combine(o_sel, o_win, ocmp, gates, out_ref, tq):
    lane = lax.broadcasted_iota(jnp.int32, (1, LANES), 1)
    lo_half = lane < HEAD_DIM
    for g in range(GQA):
        def pick(a):
            return jnp.where(lo_half, a[g * tq:(g + 1) * tq], a[(GQA + g) * tq:(GQA + g + 1) * tq])

        def gate(br):
            c0 = br * N_SLAB + g
            c1 = br * N_SLAB + GQA + g
            return jnp.where(lo_half, gates[:, c0:c0 + 1], gates[:, c1:c1 + 1])

        out_ref[:, g * LANES:(g + 1) * LANES] = (gate(0) * ocmp[:, g * LANES:(g + 1) * LANES]
                                                 + gate(1) * pick(o_sel) + gate(2) * pick(o_win))


def _nsa_prompt_body(cnt_ref, lst_ref, q_ref, gates_ref, ocmp_ref, selb_ref, selkv_ref, winkv_ref, oh_ref,
                     out_ref, qaug_ref, m_ref, acc_ref, *, tq, tk, nqb, nkt):
    b = pl.program_id(0)
    qb = pl.program_id(1)
    t0 = pl.multiple_of(qb * tq, tq)
    _build_qaug(q_ref[...], selb_ref, qaug_ref, tq)
    m_ref[...] = jnp.full(m_ref.shape, NEG, F32)
    acc_ref[...] = jnp.zeros(acc_ref.shape, F32)
    qrel = lax.broadcasted_iota(jnp.int32, (tq, 1), 0)
    lo_half = lax.broadcasted_iota(jnp.int32, (1, LANES), 1) < HEAD_DIM
    one = jnp.ones((), BF16)
    cd = t0 // tk

    def tile(c, kv, causal):
        k0 = pl.multiple_of(c * tk, tk)
        kaug = jnp.concatenate([selkv_ref[pl.ds(k0, tk), 0:LANES], oh_ref[pl.ds(k0, tk), :]], axis=1)
        vt = selkv_ref[pl.ds(k0, tk), LANES:2 * LANES]
        vaug = jnp.where(lo_half, vt, one) if kv == 0 else jnp.where(lo_half, one, vt)
        r0 = kv * GQA * tq
        s = lax.dot_general(qaug_ref[r0:r0 + GQA * tq], kaug, NT_DIMS, preferred_element_type=F32)
        kposrel = (k0 - t0) + lax.broadcasted_iota(jnp.int32, (1, tk), 1)
        _flash_update(s, vaug, kposrel, qrel if causal else None, m_ref, acc_ref, tq, kv, False)

    for kv in range(KV_NSA):
        tile(cd, kv, True)

        slot = (b * nqb + qb) * KV_NSA + kv
        n = cnt_ref[slot]

        _unrolled_loop(n, lambda i, kv=kv, slot=slot: tile(lst_ref[slot * nkt + i], kv, False), UNROLL_NSA)

    o_sel = _normalised(acc_ref)
    nw = tq + WINDOW
    kw = winkv_ref[pl.ds(t0, nw), 0:LANES]
    vw = winkv_ref[pl.ds(t0, nw), LANES:2 * LANES]
    valid_fn = lambda j, r: (j >= r) & (j <= r + WINDOW) & (j + t0 >= WINDOW)
    o_win = _window_attn(qaug_ref[:, 0:LANES], kw, vw, valid_fn, tq)
    _combine(o_sel, o_win, ocmp_ref[...], gates_ref[...], out_ref, tq)


def _nsa_prompt(counts, lists, q_a, gates, ocmp, selb, selkv, winkv_pad, onehot):
    nb, s, _ = q_a.shape
    tq, tk = TQ_NSA, TK_NSA
    nqb, nkt = s // tq, s // tk
    rows = N_SLAB * tq
    qmap = lambda b, i, c, l: (b, i, 0)
    grid_spec = pltpu.PrefetchScalarGridSpec(
        num_scalar_prefetch=2,
        grid=(nb, nqb),
        in_specs=[
            pl.BlockSpec((None, tq, 4 * LANES), qmap),
            pl.BlockSpec((None, tq, LANES), qmap),
            pl.BlockSpec((None, tq, 4 * LANES), qmap),
            pl.BlockSpec((None, KV_NSA, tq, LANES), lambda b, i, c, l: (b, 0, i, 0)),
            pl.BlockSpec((None, s, 2 * LANES), lambda b, i, c, l: (b, 0, 0)),
            pl.BlockSpec((None, s + WINDOW, 2 * LANES), lambda b, i, c, l: (b, 0, 0)),
            pl.BlockSpec((s, LANES), lambda b, i, c, l: (0, 0)),
        ],
        out_specs=pl.BlockSpec((None, tq, 4 * LANES), qmap),
        scratch_shapes=[
            pltpu.VMEM((rows, 2 * LANES), BF16),
            pltpu.VMEM((rows, LANES), F32),
            pltpu.VMEM((rows, LANES), F32),
        ],
    )
    return pl.pallas_call(
        functools.partial(_nsa_prompt_body, tq=tq, tk=tk, nqb=nqb, nkt=nkt),
        out_shape=jax.ShapeDtypeStruct((nb, s, 4 * LANES), F32),
        grid_spec=grid_spec,
        compiler_params=_cparams(("parallel", "arbitrary")),
        name="nsa_prompt",
    )(counts, lists, q_a, gates, ocmp, selb, selkv, winkv_pad, onehot)


def _sb_prompt_body(q_ref, k_ref, v_ref, u_ref, out_ref, qs_ref, r_ref, acc_ref, *, tq):
    qb = pl.program_id(2)
    lane = lax.broadcasted_iota(jnp.int32, (1, LANES), 1)
    lo_half = lane < HEAD_DIM
    q = q_ref[...]
    zero = jnp.zeros((), BF16)
    qs_ref[0:tq] = jnp.where(lo_half, q, zero)
    qs_ref[tq:2 * tq] = jnp.where(lo_half, zero, q)
    r_ref[...] = jnp.zeros(r_ref.shape, F32)
    acc_ref[...] = jnp.zeros(acc_ref.shape, F32)
    nh = tq // LANES

    def tile(c, diag):
        k0 = pl.multiple_of(c * tq, tq)
        vt = v_ref[pl.ds(k0, tq), :]
        umat = u_ref[...]
        z_all = lax.dot_general(qs_ref[...], k_ref[pl.ds(k0, tq), :], NT_DIMS, preferred_element_type=F32)
        r_all = r_ref[...]
        a_parts, t_parts = [], []
        for sl in range(2 * tq // RS_SB):
            rows = slice(sl * RS_SB, (sl + 1) * RS_SB)
            z = z_all[rows]
            sp = _softplus2(z)
            if diag:
                qi = (sl * RS_SB) % tq + lax.broadcasted_iota(jnp.int32, (RS_SB, 1), 0)
                before = lax.broadcasted_iota(jnp.int32, (1, tq), 1) < qi
                nl = jnp.where(before, sp, 0.0)
            else:
                nl = sp
            wn = jnp.dot(nl.astype(BF16), umat, preferred_element_type=F32)
            r = r_all[rows]
            parts = []
            for h in range(nh):
                cs = slice(h * LANES, (h + 1) * LANES)
                parts.append(jnp.exp2((z[:, cs] - sp[:, cs]) - (r + wn[:, cs])))
            a = jnp.concatenate(parts, axis=1)
            if diag:
                a = jnp.where(before, a, 0.0)
            a_parts.append(a.astype(BF16))
            t_parts.append(jnp.broadcast_to(wn[:, 0:1] + nl[:, 0:1], (RS_SB, LANES)))
        acc_ref[...] += jnp.dot(jnp.concatenate(a_parts, axis=0), vt, preferred_element_type=F32)
        r_ref[...] = r_all + jnp.concatenate(t_parts, axis=0)

    tile(qb, True)

    _unrolled_loop(qb, lambda i: tile(qb - 1 - i, False), UNROLL_SB)
    out_ref[...] = jnp.where(lo_half, acc_ref[0:tq], acc_ref[tq:2 * tq])


def _sb_prompt(q_b, sbb, umat):
    nb, s, _ = q_b.shape
    tq = TQ_SB
    npair = q_b.shape[2] // LANES
    return pl.pallas_call(
        functools.partial(_sb_prompt_body, tq=tq),
        out_shape=jax.ShapeDtypeStruct((nb, s, npair * LANES), F32),
        grid=(nb, npair, s // tq),
        in_specs=[
            pl.BlockSpec((None, tq, LANES), lambda b, j, i: (b, i, j)),
            pl.BlockSpec((None, s, LANES), lambda b, j, i: (b, 0, j)),
            pl.BlockSpec((None, s, LANES), lambda b, j, i: (b, 0, npair + j)),
            pl.BlockSpec((tq, tq), lambda b, j, i: (0, 0)),
        ],
        out_specs=pl.BlockSpec((None, tq, LANES), lambda b, j, i: (b, i, j)),
        scratch_shapes=[
            pltpu.VMEM((2 * tq, LANES), BF16),
            pltpu.VMEM((2 * tq, LANES), F32),
            pltpu.VMEM((2 * tq, LANES), F32),
        ],
        compiler_params=_cparams(("parallel", "parallel", "arbitrary")),
        name="sb_prompt",
    )(q_b, sbb, sbb, umat)


def _nsa_sample_body(pt_ref, *refs, t, npages, past):
    pages = refs[:P_PAGES]
    (qaug_ref, new_ref, oh_ref, winkv_ref, gates_ref, ocmp_ref, out_ref, m_ref, acc_ref) = refs[P_PAGES:]
    step = pl.program_id(1)
    nsteps = npages // P_PAGES
    tk = P_PAGES * PAGE
    lo_rows = lax.broadcasted_iota(jnp.int32, (LANES, 1), 0) < HEAD_DIM
    one = jnp.ones((), BF16)

    def update(s, vt, kposrel, qrel):
        for kv in range(KV_NSA):
            vaug = jnp.where(lo_rows, vt, one) if kv == 0 else jnp.where(lo_rows, one, vt)
            _flash_update(s[kv * GQA * t:(kv + 1) * GQA * t], vaug, kposrel, qrel, m_ref, acc_ref, t, kv, True)

    @pl.when(step == 0)
    def _():
        m_ref[...] = jnp.full(m_ref.shape, NEG, F32)
        acc_ref[...] = jnp.zeros(acc_ref.shape, F32)
        s = jnp.dot(qaug_ref[:, 0:LANES], new_ref[0:LANES, :], preferred_element_type=F32)
        j = lax.broadcasted_iota(jnp.int32, (1, LANES), 1)
        s = jnp.where(j < t, s, NEG)
        update(s, new_ref[LANES:2 * LANES, :], j, lax.broadcasted_iota(jnp.int32, (t, 1), 0))

    pg0 = npages - (step + 1) * P_PAGES
    k0 = pl.multiple_of(pg0 * PAGE, tk)
    kt = jnp.concatenate([r[0:LANES, :].astype(BF16) for r in pages], axis=1)
    vt = jnp.concatenate([r[LANES:2 * LANES, :].astype(BF16) for r in pages], axis=1)
    kaug = jnp.concatenate([kt, oh_ref[:, pl.ds(k0, tk)]], axis=0)
    s = jnp.dot(qaug_ref[...], kaug, preferred_element_type=F32)
    kposrel = (k0 - past) + lax.broadcasted_iota(jnp.int32, (1, tk), 1)
    update(s, vt, kposrel, None)

    @pl.when(step == nsteps - 1)
    def _():
        o_sel = _normalised(acc_ref)
        valid_fn = lambda j, r: (j >= r) & (j <= r + WINDOW) & (j < WINDOW + t)
        o_win = _window_attn(qaug_ref[:, 0:LANES], winkv_ref[:, 0:LANES], winkv_ref[:, LANES:2 * LANES],
                             valid_fn, t)
        _combine(o_sel, o_win, ocmp_ref[...], gates_ref[...], out_ref, t)


def _nsa_sample(cache_t, page_table, qaug, selnew_t, onehot_t, winkv, gates, ocmp, *, past):
    nseq, npages = page_table.shape
    t = gates.shape[1]
    rows = N_SLAB * t
    tk = P_PAGES * PAGE

    def page_map(i):
        return lambda s, j, pt: (pt[s * npages + npages - (j + 1) * P_PAGES + i], 1, 0)

    seq = lambda s, j, pt: (s, 0, 0)
    grid_spec = pltpu.PrefetchScalarGridSpec(
        num_scalar_prefetch=1,
        grid=(nseq, npages // P_PAGES),
        in_specs=[pl.BlockSpec((None, 2 * LANES, PAGE), page_map(i)) for i in range(P_PAGES)] + [
            pl.BlockSpec((None, rows, 2 * LANES), seq),
            pl.BlockSpec((None, 2 * LANES, LANES), seq),
            pl.BlockSpec(onehot_t.shape, lambda s, j, pt: (0, 0)),
            pl.BlockSpec((None, winkv.shape[1], 2 * LANES), seq),
            pl.BlockSpec((None, t, LANES), seq),
            pl.BlockSpec((None, t, 4 * LANES), seq),
        ],
        out_specs=pl.BlockSpec((None, t, 4 * LANES), seq),
        scratch_shapes=[
            pltpu.VMEM((rows, LANES), F32),
            pltpu.VMEM((rows, LANES), F32),
        ],
    )
    return pl.pallas_call(
        functools.partial(_nsa_sample_body, t=t, npages=npages, past=past),
        out_shape=jax.ShapeDtypeStruct((nseq, t, 4 * LANES), F32),
        grid_spec=grid_spec,
        compiler_params=_cparams(("parallel", "arbitrary")),
        name="nsa_sample",
    )(page_table.reshape(-1), *([cache_t] * P_PAGES), qaug, selnew_t, onehot_t, winkv, gates, ocmp)


def _sb_sample_body(pt_ref, *refs, t, nh):
    pages = refs[:P_PAGES]
    qbd_ref, new_ref, uj_ref, out_ref, r_ref, acc_ref = refs[P_PAGES:]
    step = pl.program_id(1)
    nsteps = pl.num_programs(1)
    d = nh * HEAD_DIM
    qbd = qbd_ref[...]
    uj = uj_ref[...]

    def chunk(z, r, before):
        sp = _softplus2(z)
        nl = sp if before is None else jnp.where(before, sp, 0.0)
        w = _split_dot(nl, uj)
        a = jnp.exp2((z - sp) - (r + w[:, 0:LANES]))
        if before is not None:
            a = jnp.where(before, a, 0.0)
        return a.astype(BF16), r + w[:, LANES:2 * LANES]

    @pl.when(step == 0)
    def _():
        rows = lax.broadcasted_iota(jnp.int32, (nh * t, LANES), 0)
        j = lax.broadcasted_iota(jnp.int32, (nh * t, LANES), 1)
        before = j < (rows & (t - 1))
        z = jnp.dot(qbd, new_ref[0:d, :], preferred_element_type=F32)
        a, r = chunk(z, jnp.zeros((nh * t, LANES), F32), before)
        r_ref[...] = r
        acc_ref[...] = lax.dot_general(a, new_ref[d:2 * d, :], NT_DIMS, preferred_element_type=F32)

    kt = jnp.concatenate([x[0:d, :].astype(BF16) for x in pages], axis=1)
    vt = jnp.concatenate([x[d:2 * d, :].astype(BF16) for x in pages], axis=1)
    z = jnp.dot(qbd, kt, preferred_element_type=F32)
    r = r_ref[...]
    parts = [None] * P_PAGES
    for i in reversed(range(P_PAGES)):
        parts[i], r = chunk(z[:, i * PAGE:(i + 1) * PAGE], r, None)
    r_ref[...] = r
    acc_ref[...] += lax.dot_general(jnp.concatenate(parts, axis=1), vt, NT_DIMS, preferred_element_type=F32)

    @pl.when(step == nsteps - 1)
    def _():
        col = jnp.right_shift(lax.broadcasted_iota(jnp.int32, (t, d), 1), 6)
        o = jnp.zeros((t, d), F32)
        for h in range(nh):
            o = o + jnp.where(col == h, acc_ref[h * t:(h + 1) * t, :], 0.0)
        out_ref[...] = o


def _sb_sample(cache_t, page_table, qbd, sbnew_t, uj, *, t):
    nseq, npages = page_table.shape
    d2 = cache_t.shape[1]
    d = d2 // 2
    nh = d // HEAD_DIM

    def page_map(i):
        return lambda s, j, pt: (pt[s * npages + npages - (j + 1) * P_PAGES + i], 0, 0)

    seq = lambda s, j, pt: (s, 0, 0)
    grid_spec = pltpu.PrefetchScalarGridSpec(
        num_scalar_prefetch=1,
        grid=(nseq, npages // P_PAGES),
        in_specs=[pl.BlockSpec((None, d2, PAGE), page_map(i)) for i in range(P_PAGES)] + [
            pl.BlockSpec((None, nh * t, d), seq),
            pl.BlockSpec((None, d2, LANES), seq),
            pl.BlockSpec(uj.shape, lambda s, j, pt: (0, 0)),
        ],
        out_specs=pl.BlockSpec((None, t, d), seq),
        scratch_shapes=[pltpu.VMEM((nh * t, LANES), F32), pltpu.VMEM((nh * t, d), F32)],
    )
    return pl.pallas_call(
        functools.partial(_sb_sample_body, t=t, nh=nh),
        out_shape=jax.ShapeDtypeStruct((nseq, t, d), F32),
        grid_spec=grid_spec,
        compiler_params=_cparams(("parallel", "arbitrary")),
        name="sb_sample",
    )(page_table.reshape(-1), *([cache_t] * P_PAGES), qbd, sbnew_t, uj)


def _outproj_body(x_ref, oa_ref, ob_ref, ga_ref, gb_ref, wa_ref, wb_ref, o_ref):
    a = _rms(oa_ref[...], ga_ref[...]).astype(BF16)
    b = _rms(ob_ref[...], gb_ref[...]).astype(BF16)
    o_ref[...] = (x_ref[...] + jnp.dot(a, wa_ref[...], preferred_element_type=F32)
                  + jnp.dot(b, wb_ref[...], preferred_element_type=F32))


def _outproj(x, oa, ob, ga, gb, wa, wb):
    n, d = x.shape
    da, db = oa.shape[1], ob.shape[1]
    tm = min(TM_FFN, n)
    const = lambda i: (0, 0)
    row = lambda i: (i, 0)
    return pl.pallas_call(
        _outproj_body,
        out_shape=jax.ShapeDtypeStruct((n, d), F32),
        grid=(n // tm,),
        in_specs=[
            pl.BlockSpec((tm, d), row), pl.BlockSpec((tm, da), row), pl.BlockSpec((tm, db), row),
            pl.BlockSpec((1, da), const), pl.BlockSpec((1, db), const),
            pl.BlockSpec((da, d), const), pl.BlockSpec((db, d), const),
        ],
        out_specs=pl.BlockSpec((tm, d), row),
        compiler_params=_cparams(("parallel",)),
        name="outproj",
    )(x, oa, ob, ga, gb, wa, wb)


def _qa_perm():
    idx = np.zeros((KV_NSA * GQA * HEAD_DIM,), np.int32)
    for kv in range(KV_NSA):
        for g in range(GQA):
            for d in range(HEAD_DIM):
                idx[LANES * g + HEAD_DIM * kv + d] = (kv * GQA + g) * HEAD_DIM + d
    return idx


def _block_rows():
    blk = np.arange(PAGE) // CMP_BLK
    rows = np.zeros((2, 16, PAGE), np.float32)
    for h in range(2):
        rows[h, 4 * h + blk, np.arange(PAGE)] = 1.0
    return rows


def _pages_feature_major(cache):
    n_pool = cache.shape[0]
    return jnp.transpose(cache, (0, 2, 3, 4, 1)).reshape(n_pool, -1, PAGE)


def _layer(xp, xs, cache_nsa, state_win, cache_sb, page_table, p):
    nb, s, d = xp.shape
    db, t, _ = xs.shape
    npages = page_table.shape[1]
    past = npages * PAGE
    d_nsa = KV_NSA * GQA * HEAD_DIM
    d_sb = p['out_norm_b'].shape[0]
    nh = d_sb // HEAD_DIM
    assert s % TK_NSA == 0 and s % TQ_SB == 0 and s // SEL_BLK <= LANES and s >= WINDOW
    assert past // SEL_BLK <= LANES and t & (t - 1) == 0 and t <= SEL_BLK and state_win.shape[1] == WINDOW
    assert npages % P_PAGES == 0 and npages % P_CMP == 0 and db % SEQ_CMP == 0 and cache_nsa.shape[1] == PAGE

    perm = _qa_perm()
    o1, o2, o3, o4 = 512, 1024, 1280, 1304
    cols = np.concatenate([perm, np.arange(o1, o3), np.arange(o4, o4 + 3 * d_sb), np.arange(o3, o4)])
    w_packed = jnp.pad(p['w_in'][:, cols], ((0, 0), (0, LANES - (o4 - o3)))).astype(BF16)
    tile2 = lambda v: jnp.tile(v, 2).reshape(1, LANES)
    qn, kn_cmp, kn_sel, kn_win = tile2(p['q_norm']), tile2(p['k_norm'][0]), tile2(p['k_norm'][1]), tile2(p['k_norm'][2])
    gidx = np.arange(LANES) // HEAD_DIM
    gsum = jnp.asarray(gidx[:, None] == gidx[None, :], BF16)
    w_sm = jax.nn.softmax(p['cmp_pos_w'].astype(F32), axis=0)
    w_cmp = jnp.tile(jnp.repeat(w_sm, HEAD_DIM, axis=1), (1, 2))
    w_cmp_t = jnp.tile(w_cmp.T, (1, PAGE // CMP_BLK))
    ga = p['out_norm_a'][perm].reshape(1, d_nsa)
    gb = p['out_norm_b'].reshape(1, d_sb)
    wo_a = p['w_o'][:d_nsa][perm].astype(BF16)
    wo_b = p['w_o'][d_nsa:].astype(BF16)
    wgu1, wd1 = p['w_ffn1_gu'].astype(BF16), p['w_ffn1_down'].astype(BF16)
    wgu2, wd2 = p['w_ffn2_gu'].astype(BF16), p['w_ffn2_down'].astype(BF16)
    g1, gm, g2 = (p[k].reshape(1, d) for k in ('norm_ffn1', 'norm_mix', 'norm_ffn2'))

    def pair_mat(n_cmp):
        return jnp.asarray(np.arange(n_cmp)[:, None] // 2 == np.arange(LANES)[None, :], BF16)

    xp2 = xp.reshape(nb * s, d)
    xs2 = xs.reshape(db * t, d)

    xp2 = _ffn(xp2, g1, wgu1, wd1)
    xs2 = _ffn(xs2, g1, wgu1, wd1)
    proj_p = _inproj(xp2, gm, w_packed, qn, kn_sel, kn_win, gsum, seq_len=s)
    proj_s = _inproj(xs2, gm, w_packed, qn, kn_sel, kn_win, gsum)
    nsa_pt, sb_pt = proj_p[1], proj_p[6]
    qa_p, _, selkv_p, win_p, winb_p, qb_p, _, sbb_p, gates_p = (
        a.reshape(nb, s, -1) if a.ndim == 2 else None for a in proj_p)
    qa_s, nsa_s, selkv_s, win_s, winb_s, qb_s, sb_s, sbb_s, gates_s = (a.reshape(db, t, -1) for a in proj_s)

    sel_rows = jnp.asarray(_block_rows(), BF16)
    ident = jnp.zeros((nb, s // PAGE), jnp.int32)
    kcvc_p = _compress_pages(nsa_pt, ident, w_cmp_t, sel_rows, kn_cmp, gsum, contiguous=True)
    ocmp_p, selb_p = _cmp_topk(qa_p, kcvc_p, pair_mat(s // CMP_BLK), nsb=1, tq=TQ_CMP, pos_base=0, k_lanes=TOP_N)
    nqb, nkt = s // TQ_NSA, s // TK_NSA
    picked = jnp.any((selb_p == 0).reshape(nb, KV_NSA, nqb, TQ_NSA, LANES), axis=3)[..., :s // SEL_BLK]
    tile_id = np.arange(nkt)
    earlier = jnp.asarray(tile_id[None, :] < (np.arange(nqb) * TQ_NSA // TK_NSA)[:, None])
    active = jnp.any(picked.reshape(nb, KV_NSA, nqb, nkt, TK_NSA // SEL_BLK), axis=-1) & earlier
    active = active.transpose(0, 2, 1, 3)
    lists = jnp.sort(jnp.where(active, tile_id, nkt), axis=-1).astype(jnp.int32).reshape(-1)
    counts = jnp.sum(active, axis=-1).astype(jnp.int32).reshape(-1)
    onehot = jnp.asarray(np.arange(s)[:, None] // SEL_BLK == np.arange(LANES)[None, :], BF16)
    winkv_pad = jnp.pad(winb_p, ((0, 0), (WINDOW, 0), (0, 0)))
    onsa_p = _nsa_prompt(counts, lists, qa_p, gates_p, ocmp_p, selb_p, selkv_p, winkv_pad, onehot)

    umat_p = jnp.asarray(np.arange(TQ_SB)[:, None] > np.arange(TQ_SB)[None, :], BF16)
    osb_p = _sb_prompt(qb_p, sbb_p, umat_p)

    cache_nsa_t = _pages_feature_major(cache_nsa)
    kcvc_s = _compress_pages(cache_nsa_t, page_table, w_cmp_t, sel_rows, kn_cmp, gsum)
    k_lanes = TOP_N - 1
    ocmp_s, selb_s = _cmp_topk(qa_s, kcvc_s, pair_mat(past // CMP_BLK), nsb=SEQ_CMP, tq=t, pos_base=past,
                               k_lanes=k_lanes)
    lane_lo = (np.arange(LANES) < HEAD_DIM)
    halfmask = jnp.asarray(np.stack([lane_lo, ~lane_lo]), BF16)
    q4 = qa_s.reshape(db, t, GQA, LANES).transpose(0, 2, 1, 3)
    qrows = q4[:, None] * halfmask[None, :, None, None, :]
    brows = jnp.broadcast_to(selb_s[:, :, None], (db, KV_NSA, GQA, t, LANES))
    qaug_s = jnp.concatenate([qrows, brows], axis=-1).reshape(db, N_SLAB * t, 2 * LANES)
    selnew_t = jnp.pad(selkv_s.transpose(0, 2, 1), ((0, 0), (0, 0), (0, LANES - t)))
    onehot_t = jnp.asarray(np.arange(LANES)[:, None] == np.arange(past)[None, :] // SEL_BLK, BF16)
    win_all = jnp.concatenate([state_win.reshape(db, WINDOW, 2 * LANES), win_s], axis=1)
    winkv_s = jnp.pad(win_all.astype(BF16), ((0, 0), (0, LANES - t), (0, 0)))
    onsa_s = _nsa_sample(cache_nsa_t, page_table, qaug_s, selnew_t, onehot_t, winkv_s, gates_s, ocmp_s, past=past)

    cache_sb_t = _pages_feature_major(cache_sb)
    eye = jnp.eye(nh, dtype=BF16)
    qbd = jnp.einsum('bthd,hg->bhtgd', qb_s.reshape(db, t, nh, HEAD_DIM), eye).reshape(db, nh * t, d_sb)
    sbnew_t = jnp.pad(sbb_s.transpose(0, 2, 1), ((0, 0), (0, 0), (0, LANES - t)))
    jj = np.arange(PAGE)
    uj = jnp.asarray(np.concatenate([jj[:, None] > jj[None, :], np.ones((PAGE, PAGE), bool)], axis=1), BF16)
    osb_s = _sb_sample(cache_sb_t, page_table, qbd, sbnew_t, uj, t=t)

    xp2 = _outproj(xp2, onsa_p.reshape(nb * s, d_nsa), osb_p.reshape(nb * s, d_sb), ga, gb, wo_a, wo_b)
    xs2 = _outproj(xs2, onsa_s.reshape(db * t, d_nsa), osb_s.reshape(db * t, d_sb), ga, gb, wo_a, wo_b)
    xp2 = _ffn(xp2, g2, wgu2, wd2)
    xs2 = _ffn(xs2, g2, wgu2, wd2)

    wb = min(WINDOW, s)
    outs = (nsa_pt.reshape(nb, 4, KV_NSA, HEAD_DIM, s).transpose(0, 4, 1, 2, 3),
            nsa_s.reshape(db, t, 4, KV_NSA, HEAD_DIM),
            win_p[:, s - wb:].reshape(nb, wb, 2, KV_NSA, HEAD_DIM),
            win_all[:, t:].reshape(db, WINDOW, 2, KV_NSA, HEAD_DIM),
            sb_pt.reshape(nb, 2, nh, HEAD_DIM, s).transpose(0, 4, 1, 2, 3),
            sb_s.reshape(db, t, 2, nh, HEAD_DIM))
    return xp2.reshape(nb, s, d), xs2.reshape(db, t, d), outs


def kernel(x_prompt, x_sample, cache_nsa, state_win, cache_sb, page_table, norm_ffn1, w_ffn1_gu, w_ffn1_down,
           norm_mix, w_in, q_norm, k_norm, cmp_pos_w, out_norm_a, out_norm_b, w_o, norm_ffn2, w_ffn2_gu,
           w_ffn2_down):
    names = ('norm_ffn1', 'w_ffn1_gu', 'w_ffn1_down', 'norm_mix', 'w_in', 'q_norm', 'k_norm', 'cmp_pos_w',
             'out_norm_a', 'out_norm_b', 'w_o', 'norm_ffn2', 'w_ffn2_gu', 'w_ffn2_down')
    params = (norm_ffn1, w_ffn1_gu, w_ffn1_down, norm_mix, w_in, q_norm, k_norm, cmp_pos_w,
              out_norm_a, out_norm_b, w_o, norm_ffn2, w_ffn2_gu, w_ffn2_down)
    depth = w_in.shape[0]
    xp, xs = x_prompt, x_sample
    per_layer = []
    for l in range(depth):
        p = {k: v[l] for k, v in zip(names, params)}
        xp, xs, outs = _layer(xp, xs, cache_nsa[l], state_win[l], cache_sb[l], page_table, p)
        per_layer.append(outs)
    stacked = tuple(jnp.stack([o[i] for o in per_layer]) for i in range(6))
    return (xp, xs) + stacked
```
